```python
import math
import jax
import jax.numpy as jnp
from jax import lax
import numpy as np

D_MODEL = 2048
BATCH = 16
SEQ = 2048
DEPTH = 1
DEC_BATCH = 2
DEC_SEQ = 16384
PAST_LEN = 128

HEAD_DIM = 128
ATTN_HEADS = 8
ATTN_KV_HEADS = 2
ATTN_GROUP = ATTN_HEADS // ATTN_KV_HEADS
ATTN_WIDTH = ATTN_HEADS * HEAD_DIM
ATTN_KV_WIDTH = ATTN_KV_HEADS * HEAD_DIM
DN_QK_HEADS = 4
DN_V_HEADS = 8
DN_DK = 128
DN_DV = 128
DN_QK_WIDTH = DN_QK_HEADS * DN_DK
DN_WIDTH = DN_V_HEADS * DN_DV
DN_CONV_DIM = 2 * DN_QK_WIDTH + DN_WIDTH
MIX_WIDTH = ATTN_WIDTH + DN_WIDTH
PROJ_SPLITS = (ATTN_WIDTH, ATTN_KV_WIDTH, ATTN_KV_WIDTH, DN_CONV_DIM, DN_WIDTH, 2 * DN_V_HEADS, 2 * DN_V_HEADS)
PROJ_DIM = sum(PROJ_SPLITS)
GRID_W = 64
ROPE_AXIS_DIM = HEAD_DIM // 2
ROPE_THETA = 10000.0
Q_BLOCK = 128
CONV_W = 5
CHUNK = 64
PEER_HEADS = 8
N_KEYS = 128
N_EXPERTS = N_KEYS * N_KEYS
PEER_QDIM = 256
PEER_HALF = PEER_QDIM // 2
PEER_TOPK = 16
PEER_TOK_BLOCK = 128
DEEPNORM_ALPHA = (2.0 * DEPTH) ** 0.25
DEEPNORM_BETA = (8.0 * DEPTH) ** -0.25
LN_EPS = 1e-5
RMS_EPS = 1e-6

kernel_name = "hymba_gqa_gdn_peer_encoder"


def _layer_norm(x, g, b):
    xf = x.astype(jnp.float32)
    mu = jnp.mean(xf, axis=-1, keepdims=True)
    var = jnp.mean(jnp.square(xf - mu), axis=-1, keepdims=True)
    y = (xf - mu) * lax.rsqrt(var + LN_EPS) * g.astype(jnp.float32) + b.astype(jnp.float32)
    return y.astype(x.dtype)


def _rms_heads(x, g):
    xf = x.astype(jnp.float32)
    y = xf * lax.rsqrt(jnp.mean(jnp.square(xf), axis=-1, keepdims=True) + RMS_EPS) * g.astype(jnp.float32)
    return y.astype(x.dtype)


def _rope_1d(seg, pos):
    dim = seg.shape[-1]
    half = dim // 2
    inv = ROPE_THETA ** (-jnp.arange(half, dtype=jnp.float32) * (2.0 / dim))
    ang = pos.astype(jnp.float32)[:, None] * inv[None, :]
    cos = jnp.cos(ang)[None, :, None, :]
    sin = jnp.sin(ang)[None, :, None, :]
    s1, s2 = seg[..., :half], seg[..., half:]
    return jnp.concatenate([s1 * cos - s2 * sin, s2 * cos + s1 * sin], axis=-1)


def _axial_rope(x, rows, cols):
    xf = x.astype(jnp.float32)
    y = jnp.concatenate([_rope_1d(xf[..., :ROPE_AXIS_DIM], rows),
                         _rope_1d(xf[..., ROPE_AXIS_DIM:], cols)], axis=-1)
    return y.astype(x.dtype)


def _block_attention(q, k, v):
    B, T = q.shape[0], q.shape[1]
    nb = T // Q_BLOCK
    qb = q.reshape(B, nb, Q_BLOCK, ATTN_KV_HEADS, ATTN_GROUP, HEAD_DIM).transpose(1, 0, 3, 4, 2, 5)
    kt = k.transpose(0, 2, 1, 3)
    vt = v.transpose(0, 2, 1, 3)
    scale = HEAD_DIM ** -0.5

    def one_block(qblk):
        s = jnp.einsum('bkgqd,bksd->bkgqs', qblk, kt).astype(jnp.float32) * scale
        p = jax.nn.softmax(s, axis=-1).astype(vt.dtype)
        return jnp.einsum('bkgqs,bksd->bkgqd', p, vt)

    o = lax.map(one_block, qb)
    return o.transpose(1, 0, 4, 2, 3, 5).reshape(B, T, ATTN_WIDTH)


def _centred_conv(x, w):
    c = x.shape[-1]
    pad = CONV_W // 2
    return lax.conv_general_dilated(x, w[:, None, :].astype(x.dtype), window_strides=(1,),
                                    padding=[(pad, pad)], dimension_numbers=('NWC', 'WIO', 'NWC'),
                                    feature_group_count=c)


def _gated_delta_chunked(q, k, v, g, beta):
    B, H, T, dk = q.shape
    dv = v.shape[-1]
    n = T // CHUNK
    q = q.reshape(B, H, n, CHUNK, dk)
    k = k.reshape(B, H, n, CHUNK, dk)
    v = v.reshape(B, H, n, CHUNK, dv)
    g = jnp.cumsum(g.reshape(B, H, n, CHUNK), axis=-1)
    beta = beta.reshape(B, H, n, CHUNK)
    incl = jnp.tril(jnp.ones((CHUNK, CHUNK), dtype=bool))
    strict = jnp.tril(jnp.ones((CHUNK, CHUNK), dtype=bool), -1)
    decay = jnp.exp(jnp.where(incl, g[..., :, None] - g[..., None, :], -jnp.inf))
    k_beta = k * beta[..., None]
    lower = jnp.where(strict, jnp.einsum('bhncd,bhnsd->bhncs', k_beta, k) * decay, 0.0)
    eye = jnp.eye(CHUNK, dtype=jnp.float32)
    t_inv = lax.linalg.triangular_solve(eye + lower, jnp.broadcast_to(eye, lower.shape),
                                        left_side=True, lower=True, unit_diagonal=True)
    g_exp = jnp.exp(g)
    u = jnp.einsum('bhncs,bhnse->bhnce', t_inv, v * beta[..., None])
    w = jnp.einsum('bhncs,bhnsd->bhncd', t_inv, k_beta * g_exp[..., None])
    a_intra = jnp.einsum('bhncd,bhnsd->bhncs', q, k) * decay
    q_dec = q * g_exp[..., None]
    g_last = g[..., -1]
    k_tail = k * jnp.exp(g_last[..., None] - g)[..., None]
    xs = (jnp.moveaxis(q_dec, 2, 0), jnp.moveaxis(w, 2, 0), jnp.moveaxis(u, 2, 0),
          jnp.moveaxis(a_intra, 2, 0), jnp.moveaxis(k_tail, 2, 0), jnp.moveaxis(g_last, 2, 0))

    def step(S, inp):
        q_c, w_c, u_c, a_c, kt_c, gl_c = inp
        v_new = u_c - jnp.einsum('bhcd,bhde->bhce', w_c, S)
        o_c = jnp.einsum('bhcd,bhde->bhce', q_c, S) + jnp.einsum('bhcs,bhse->bhce', a_c, v_new)
        S = S * jnp.exp(gl_c)[..., None, None] + jnp.einsum('bhcd,bhce->bhde', kt_c, v_new)
        return S, o_c

    s0 = jnp.zeros((B, H, dk, dv), jnp.float32)
    _, o = lax.scan(step, s0, xs)
    return jnp.moveaxis(o, 0, 2).reshape(B, H, T, dv)


def _bidir_gated_deltanet(dqkv, dz, db, da, conv_w, a_log, dt_bias, norm_w):
    B, T, _ = dqkv.shape
    h = jax.nn.silu(_centred_conv(dqkv, conv_w)).astype(jnp.float32)
    dq, dk, dv = jnp.split(h, [DN_QK_WIDTH, 2 * DN_QK_WIDTH], axis=-1)
    rep = DN_V_HEADS // DN_QK_HEADS
    dq = jnp.repeat(dq.reshape(B, T, DN_QK_HEADS, DN_DK), rep, axis=2)
    dk = jnp.repeat(dk.reshape(B, T, DN_QK_HEADS, DN_DK), rep, axis=2)
    dv = dv.reshape(B, T, DN_V_HEADS, DN_DV)
    dq = dq * lax.rsqrt(jnp.sum(dq * dq, axis=-1, keepdims=True) + RMS_EPS) * (DN_DK ** -0.5)
    dk = dk * lax.rsqrt(jnp.sum(dk * dk, axis=-1, keepdims=True) + RMS_EPS)
    q = dq.transpose(0, 2, 1, 3)
    k = dk.transpose(0, 2, 1, 3)
    v = dv.transpose(0, 2, 1, 3)
    beta = jax.nn.sigmoid(db.astype(jnp.float32)).reshape(B, T, 2, DN_V_HEADS)
    g = -jnp.exp(a_log.astype(jnp.float32)) * jax.nn.softplus(
        da.astype(jnp.float32).reshape(B, T, 2, DN_V_HEADS) + dt_bias.astype(jnp.float32))
    beta = beta.transpose(2, 0, 3, 1)
    g = g.transpose(2, 0, 3, 1)
    o_fwd = _gated_delta_chunked(q, k, v, g[0], beta[0])
    o_bwd = jnp.flip(_gated_delta_chunked(jnp.flip(q, 2), jnp.flip(k, 2), jnp.flip(v, 2),
                                          jnp.flip(g[1], 2), jnp.flip(beta[1], 2)), 2)
    o = (o_fwd + o_bwd).transpose(0, 2, 1, 3)
    o = o * lax.rsqrt(jnp.mean(o * o, axis=-1, keepdims=True) + RMS_EPS) * norm_w.astype(jnp.float32)
    o = o * jax.nn.silu(dz.astype(jnp.float32).reshape(B, T, DN_V_HEADS, DN_DV))
    return o.reshape(B, T, DN_WIDTH).astype(dqkv.dtype)


def _peer(x, wq, sub_keys, exp_u, exp_v):
    B, T, D = x.shape
    m = B * T
    xf = x.reshape(m, D)
    qry = (xf @ wq).astype(jnp.float32).reshape(m, PEER_HEADS, 2, PEER_HALF)
    s = jnp.einsum('mhpd,hpkd->mhpk', qry, sub_keys.astype(jnp.float32))
    s_top, i_top = lax.top_k(s, PEER_TOPK)
    cand_s = (s_top[:, :, 0, :, None] + s_top[:, :, 1, None, :]).reshape(m, PEER_HEADS, PEER_TOPK * PEER_TOPK)
    cand_i = (i_top[:, :, 0, :, None] * N_KEYS + i_top[:, :, 1, None, :]).reshape(m, PEER_HEADS, PEER_TOPK * PEER_TOPK)
    sel_s, pos = lax.top_k(cand_s, PEER_TOPK)
    idx = jnp.take_along_axis(cand_i, pos, axis=-1)
    gate = jax.nn.softmax(sel_s, axis=-1).astype(x.dtype)
    nb = m // PEER_TOK_BLOCK

    def one_block(args):
        xb, ib, gb = args
        u = jnp.take(exp_u, ib, axis=0)
        act = jax.nn.gelu(jnp.einsum('tnkd,td->tnk', u, xb), approximate=False) * gb
        vv = jnp.take(exp_v, ib, axis=0)
        return jnp.einsum('tnk,tnkd->td', act, vv)

    y = lax.map(one_block, (xf.reshape(nb, PEER_TOK_BLOCK, D),
                            idx.reshape(nb, PEER_TOK_BLOCK, PEER_HEADS, PEER_TOPK),
                            gate.reshape(nb, PEER_TOK_BLOCK, PEER_HEADS, PEER_TOPK)))
    return y.reshape(B, T, D)


def _layer(x, w_in, q_norm_w, k_norm_w, dn_conv_w, dn_a_log, dn_dt_bias, dn_norm_w, w_out,
           ln1_g, ln1_b, peer_wq, peer_keys, peer_u, peer_v, ln2_g, ln2_b):
    B, T, _ = x.shape
    n_rows = T // GRID_W
    rows = jnp.repeat(jnp.arange(n_rows, dtype=jnp.int32), GRID_W)
    cols = jnp.arange(T, dtype=jnp.int32) % GRID_W
    proj = x @ w_in
    offsets = [int(o) for o in np.cumsum(PROJ_SPLITS)[:-1]]
    aq, ak, av, dqkv, dz, db, da = jnp.split(proj, offsets, axis=-1)
    q = _axial_rope(_rms_heads(aq.reshape(B, T, ATTN_HEADS, HEAD_DIM), q_norm_w), rows, cols)
    k = _axial_rope(_rms_heads(ak.reshape(B, T, ATTN_KV_HEADS, HEAD_DIM), k_norm_w), rows, cols)
    v = av.reshape(B, T, ATTN_KV_HEADS, HEAD_DIM)
    o_attn = _block_attention(q, k, v)
    o_dn = _bidir_gated_deltanet(dqkv, dz, db, da, dn_conv_w, dn_a_log, dn_dt_bias, dn_norm_w)
    mix = jnp.concatenate([o_attn, o_dn.astype(o_attn.dtype)], axis=-1) @ w_out
    x = _layer_norm(DEEPNORM_ALPHA * x + mix, ln1_g, ln1_b)
    x = _layer_norm(DEEPNORM_ALPHA * x + _peer(x, peer_wq, peer_keys, peer_u, peer_v), ln2_g, ln2_b)
    return x


def setup_inputs(seed: int = 0) -> dict:
    key = jax.random.key(seed)
    ks = jax.random.split(key, 24)
    f32 = jnp.float32
    nrm = jax.random.normal
    D = D_MODEL
    x_prompt = nrm(ks[0], (BATCH, SEQ, D), f32)
    x_sample = nrm(ks[1], (DEC_BATCH, DEC_SEQ, D), f32)
    ln_in_g = 1.0 + 0.02 * nrm(ks[2], (D,), f32)
    ln_in_b = 0.02 * nrm(ks[3], (D,), f32)
    col_scale = jnp.concatenate([
        jnp.ones((ATTN_WIDTH + ATTN_KV_WIDTH,), f32),
        jnp.full((ATTN_KV_WIDTH,), DEEPNORM_BETA, f32),
        jnp.ones((2 * DN_QK_WIDTH,), f32),
        jnp.full((DN_WIDTH,), DEEPNORM_BETA, f32),
        jnp.ones((DN_WIDTH + 4 * DN_V_HEADS,), f32)])
    w_in = nrm(ks[4], (DEPTH, D, PROJ_DIM), f32) * (D ** -0.5) * col_scale
    q_norm_w = 1.0 + 0.02 * nrm(ks[5], (DEPTH, HEAD_DIM), f32)
    k_norm_w = 1.0 + 0.02 * nrm(ks[6], (DEPTH, HEAD_DIM), f32)
    dn_conv_w = nrm(ks[7], (DEPTH, CONV_W, DN_CONV_DIM), f32) * (CONV_W ** -0.5)
    dn_a_log = jnp.log(jax.random.uniform(ks[8], (DEPTH, 2, DN_V_HEADS), f32, 1.0, 16.0))
    dt = jnp.exp(jax.random.uniform(ks[9], (DEPTH, 2, DN_V_HEADS), f32, math.log(1e-3), math.log(1e-1)))
    dn_dt_bias = dt + jnp.log(-jnp.expm1(-dt))
    dn_norm_w = 1.0 + 0.02 * nrm(ks[10], (DEPTH, DN_DV), f32)
    w_out = nrm(ks[11], (DEPTH, MIX_WIDTH, D), f32) * (MIX_WIDTH ** -0.5) * DEEPNORM_BETA
    ln1_g = 1.0 + 0.02 * nrm(ks[12], (DEPTH, D), f32)
    ln1_b = 0.02 * nrm(ks[13], (DEPTH, D), f32)
    peer_wq = nrm(ks[14], (DEPTH, D, PEER_HEADS * PEER_QDIM), f32) * (D ** -0.5)
    peer_keys = nrm(ks[15], (DEPTH, PEER_HEADS, 2, N_KEYS, PEER_HALF), f32) * (PEER_HALF ** -0.5)
    peer_u = nrm(ks[16], (DEPTH, N_EXPERTS, D), f32) * (D ** -0.5)
    peer_v = nrm(ks[17], (DEPTH, N_EXPERTS, D), f32) * DEEPNORM_BETA
    ln2_g = 1.0 + 0.02 * nrm(ks[18], (DEPTH, D), f32)
    ln2_b = 0.02 * nrm(ks[19], (DEPTH, D), f32)
    return {"x_prompt": x_prompt, "x_sample": x_sample, "ln_in_g": ln_in_g, "ln_in_b": ln_in_b,
            "w_in": w_in, "q_norm_w": q_norm_w, "k_norm_w": k_norm_w, "dn_conv_w": dn_conv_w,
            "dn_a_log": dn_a_log, "dn_dt_bias": dn_dt_bias, "dn_norm_w": dn_norm_w, "w_out": w_out,
            "ln1_g": ln1_g, "ln1_b": ln1_b, "peer_wq": peer_wq, "peer_keys": peer_keys,
            "peer_u": peer_u, "peer_v": peer_v, "ln2_g": ln2_g, "ln2_b": ln2_b}


def reference(x_prompt, x_sample, ln_in_g, ln_in_b, w_in, q_norm_w, k_norm_w, dn_conv_w,
              dn_a_log, dn_dt_bias, dn_norm_w, w_out, ln1_g, ln1_b, peer_wq, peer_keys,
              peer_u, peer_v, ln2_g, ln2_b):
    def trunk(x):
        x = _layer_norm(x, ln_in_g, ln_in_b)
        for l in range(DEPTH):
            x = _layer(x, w_in[l], q_norm_w[l], k_norm_w[l], dn_conv_w[l], dn_a_log[l], dn_dt_bias[l],
                       dn_norm_w[l], w_out[l], ln1_g[l], ln1_b[l], peer_wq[l], peer_keys[l],
                       peer_u[l], peer_v[l], ln2_g[l], ln2_b[l])
        return x

    y_prompt = trunk(x_prompt)
    y_sample = trunk(x_sample)
    return (y_prompt, y_sample)
```

```python
import functools
import math

import jax
import jax.numpy as jnp
from jax import lax
from jax.experimental import pallas as pl
from jax.experimental.pallas import tpu as pltpu

F32 = jnp.float32
BF16 = jnp.bfloat16

HEAD_DIM = 128
ATTN_HEADS = 8
ATTN_KV_HEADS = 2
ATTN_GROUP = ATTN_HEADS // ATTN_KV_HEADS
ATTN_WIDTH = ATTN_HEADS * HEAD_DIM
ATTN_KV_WIDTH = ATTN_KV_HEADS * HEAD_DIM
DN_QK_HEADS = 4
DN_V_HEADS = 8
DN_DK = 128
DN_DV = 128
DN_QK_WIDTH = DN_QK_HEADS * DN_DK
DN_WIDTH = DN_V_HEADS * DN_DV
DN_CONV_DIM = 2 * DN_QK_WIDTH + DN_WIDTH
GRID_W = 64
ROPE_THETA = 10000.0
CONV_W = 5
PEER_HEADS = 8
PEER_TOPK = 16
DEPTH = 1
DEEPNORM_ALPHA = (2.0 * DEPTH) ** 0.25
LN_EPS = 1e-5
RMS_EPS = 1e-6

LANES = 128
SUBLANES = 8
VMEM_LIMIT_BYTES = 56 * 1024 * 1024

MAIN_WIDTH = ATTN_WIDTH + DN_WIDTH + DN_CONV_DIM + 2 * ATTN_KV_WIDTH
TAIL_WIDTH = LANES
DN_CHUNK = 128

NT_DIMS = (((1,), (1,)), ((), ()))
TN_DIMS = (((0,), (0,)), ((), ()))


def _cparams(sem):
    return pltpu.CompilerParams(dimension_semantics=sem, vmem_limit_bytes=VMEM_LIMIT_BYTES)


def _layer_norm(x, g, b):
    mu = jnp.mean(x, axis=-1, keepdims=True)
    xc = x - mu
    var = jnp.mean(xc * xc, axis=-1, keepdims=True)
    return xc * lax.rsqrt(var + LN_EPS) * g + b


def _sigmoid(x):
    return 1.0 / (1.0 + jnp.exp(-x))


def _bdot(a, b, dims=None):
    a = a.astype(BF16)
    b = b.astype(BF16)
    if dims is None:
        return jnp.dot(a, b, preferred_element_type=F32)
    return lax.dot_general(a, b, dims, preferred_element_type=F32)


def _inproj_kernel(x_ref, g_ref, b_ref, w_ref, wt_ref, o_ref, ot_ref, xn_ref):
    @pl.when(pl.program_id(1) == 0)
    def _():
        xn = _layer_norm(x_ref[...], g_ref[...], b_ref[...]).astype(BF16)
        xn_ref[...] = xn
        ot_ref[...] = jnp.dot(xn, wt_ref[...], preferred_element_type=F32)

    o_ref[...] = jnp.dot(xn_ref[...], w_ref[...], preferred_element_type=F32)


def _inproj(x, g, b, w_main, w_tail):
    m, d = x.shape
    tm = min(1024, m)
    tn = 512
    return pl.pallas_call(
        _inproj_kernel,
        grid=(m // tm, MAIN_WIDTH // tn),
        in_specs=[
            pl.BlockSpec((tm, d), lambda i, j: (i, 0)),
            pl.BlockSpec((1, d), lambda i, j: (0, 0)),
            pl.BlockSpec((1, d), lambda i, j: (0, 0)),
            pl.BlockSpec((d, tn), lambda i, j: (0, j)),
            pl.BlockSpec((d, TAIL_WIDTH), lambda i, j: (0, 0)),
        ],
        out_specs=[
            pl.BlockSpec((tm, tn), lambda i, j: (i, j)),
            pl.BlockSpec((tm, TAIL_WIDTH), lambda i, j: (i, 0)),
        ],
        out_shape=[
            jax.ShapeDtypeStruct((m, MAIN_WIDTH), F32),
            jax.ShapeDtypeStruct((m, TAIL_WIDTH), F32),
        ],
        scratch_shapes=[pltpu.VMEM((tm, d), BF16)],
        compiler_params=_cparams(("parallel", "arbitrary")),
        name="inproj",
    )(x, g, b, w_main, w_tail)


def _qkprep_kernel(aq_ref, ak_ref, av_ref, cos_ref, sin_ref, qw_ref, kw_ref, q_ref, k_ref, v_ref):
    cos = cos_ref[...]
    sin = sin_ref[...]
    lane = lax.broadcasted_iota(jnp.int32, cos.shape, 1)
    first = (lane % (HEAD_DIM // 2)) < (HEAD_DIM // 4)

    def prep(x, w, scale):
        ms = jnp.mean(x * x, axis=-1, keepdims=True)
        y = x * lax.rsqrt(ms + RMS_EPS) * w
        sw = jnp.where(first, pltpu.roll(y, LANES - HEAD_DIM // 4, 1), pltpu.roll(y, HEAD_DIM // 4, 1))
        r = y * cos + sw * sin
        return (r * scale).astype(BF16)

    for h in range(ATTN_HEADS):
        sl = slice(h * HEAD_DIM, (h + 1) * HEAD_DIM)
        q_ref[:, sl] = prep(aq_ref[:, sl], qw_ref[...], HEAD_DIM ** -0.5)
    for h in range(ATTN_KV_HEADS):
        sl = slice(h * HEAD_DIM, (h + 1) * HEAD_DIM)
        k_ref[:, sl] = prep(ak_ref[:, sl], kw_ref[...], 1.0)
    v_ref[...] = av_ref[...].astype(BF16)


def _rope_tables(t):
    half = HEAD_DIM // 4
    inv = ROPE_THETA ** (-jnp.arange(half, dtype=F32) * (2.0 / (HEAD_DIM // 2)))
    pos = jnp.arange(t, dtype=jnp.int32)
    rows = (pos // GRID_W).astype(F32)[:, None] * inv[None, :]
    cols = (pos % GRID_W).astype(F32)[:, None] * inv[None, :]
    cos = jnp.concatenate([jnp.cos(rows), jnp.cos(rows), jnp.cos(cols), jnp.cos(cols)], axis=-1)
    sin = jnp.concatenate([-jnp.sin(rows), jnp.sin(rows), -jnp.sin(cols), jnp.sin(cols)], axis=-1)
    return cos, sin


def _qkprep(proj, t, q_norm_w, k_norm_w):
    m = proj.shape[0]
    tm = min(512, t)
    nt = t // tm
    cos, sin = _rope_tables(t)
    kv_blk = (MAIN_WIDTH - 2 * ATTN_KV_WIDTH) // ATTN_KV_WIDTH
    return pl.pallas_call(
        _qkprep_kernel,
        grid=(m // tm,),
        in_specs=[
            pl.BlockSpec((tm, ATTN_WIDTH), lambda i: (i, 0)),
            pl.BlockSpec((tm, ATTN_KV_WIDTH), lambda i: (i, kv_blk)),
            pl.BlockSpec((tm, ATTN_KV_WIDTH), lambda i: (i, kv_blk + 1)),
            pl.BlockSpec((tm, HEAD_DIM), lambda i: (i % nt, 0)),
            pl.BlockSpec((tm, HEAD_DIM), lambda i: (i % nt, 0)),
            pl.BlockSpec((1, HEAD_DIM), lambda i: (0, 0)),
            pl.BlockSpec((1, HEAD_DIM), lambda i: (0, 0)),
        ],
        out_specs=[
            pl.BlockSpec((tm, ATTN_WIDTH), lambda i: (i, 0)),
            pl.BlockSpec((tm, ATTN_KV_WIDTH), lambda i: (i, 0)),
            pl.BlockSpec((tm, ATTN_KV_WIDTH), lambda i: (i, 0)),
        ],
        out_shape=[
            jax.ShapeDtypeStruct((m, ATTN_WIDTH), BF16),
            jax.ShapeDtypeStruct((m, ATTN_KV_WIDTH), BF16),
            jax.ShapeDtypeStruct((m, ATTN_KV_WIDTH), BF16),
        ],
        compiler_params=_cparams(("parallel",)),
        name="qkprep",
    )(proj, proj, proj, cos, sin, q_norm_w.reshape(1, HEAD_DIM), k_norm_w.reshape(1, HEAD_DIM))


def _attn_kernel(q_ref, k_ref, v_ref, o_ref, qs_ref, m_ref, l_ref, acc_ref, *, tq, tk, nk):
    for g in range(ATTN_GROUP):
        qs_ref[g * tq:(g + 1) * tq, :] = q_ref[:, g * HEAD_DIM:(g + 1) * HEAD_DIM]
    m_ref[...] = jnp.full(m_ref.shape, -jnp.inf, F32)
    l_ref[...] = jnp.zeros(l_ref.shape, F32)
    acc_ref[...] = jnp.zeros(acc_ref.shape, F32)

    def body(c, carry):
        off = pl.multiple_of(c * tk, tk)
        k = k_ref[pl.ds(off, tk), :]
        v = v_ref[pl.ds(off, tk), :]
        s = lax.dot_general(qs_ref[...], k, NT_DIMS, preferred_element_type=F32)
        m_prev = m_ref[...]
        m_new = jnp.maximum(m_prev, jnp.max(s, axis=-1, keepdims=True))
        alpha = jnp.exp(m_prev - m_new)
        p = jnp.exp(s - m_new)
        l_ref[...] = alpha * l_ref[...] + jnp.sum(p, axis=-1, keepdims=True)
        acc_ref[...] = acc_ref[...] * alpha + jnp.dot(p.astype(BF16), v, preferred_element_type=F32)
        m_ref[...] = m_new
        return carry

    lax.fori_loop(0, nk, body, 0)
    out = acc_ref[...] / l_ref[...]
    for g in range(ATTN_GROUP):
        o_ref[:, g * HEAD_DIM:(g + 1) * HEAD_DIM] = out[g * tq:(g + 1) * tq, :].astype(o_ref.dtype)


def _attention(q, k, v, b, t):
    tq = min(256, t)
    tk = min(512, t)
    gw = ATTN_GROUP * HEAD_DIM
    q3 = q.reshape(b, t, ATTN_WIDTH)
    k3 = k.reshape(b, t, ATTN_KV_WIDTH)
    v3 = v.reshape(b, t, ATTN_KV_WIDTH)
    out = pl.pallas_call(
        functools.partial(_attn_kernel, tq=tq, tk=tk, nk=t // tk),
        grid=(b, ATTN_KV_HEADS, t // tq),
        in_specs=[
            pl.BlockSpec((None, tq, gw), lambda bi, kv, qi: (bi, qi, kv)),
            pl.BlockSpec((None, t, HEAD_DIM), lambda bi, kv, qi: (bi, 0, kv)),
            pl.BlockSpec((None, t, HEAD_DIM), lambda bi, kv, qi: (bi, 0, kv)),
        ],
        out_specs=pl.BlockSpec((None, tq, gw), lambda bi, kv, qi: (bi, qi, kv)),
        out_shape=jax.ShapeDtypeStruct((b, t, ATTN_WIDTH), BF16),
        scratch_shapes=[
            pltpu.VMEM((ATTN_GROUP * tq, HEAD_DIM), BF16),
            pltpu.VMEM((ATTN_GROUP * tq, 1), F32),
            pltpu.VMEM((ATTN_GROUP * tq, 1), F32),
            pltpu.VMEM((ATTN_GROUP * tq, HEAD_DIM), F32),
        ],
        compiler_params=_cparams(("parallel", "parallel", "arbitrary")),
        name="attn",
    )(q3, k3, v3)
    return out.reshape(b * t, ATTN_WIDTH)


def _dnprep_kernel(x_ref, prev_ref, next_ref, cw_ref, q_ref, k_ref, v_ref, xe_ref, *, tt, nt):
    i = pl.program_id(1)
    halo = SUBLANES
    pad = CONV_W // 2
    xe_ref[0:halo, :] = jnp.where(i > 0, prev_ref[...], 0.0)
    xe_ref[halo:halo + tt, :] = x_ref[...]
    xe_ref[halo + tt:halo + tt + halo, :] = jnp.where(i < nt - 1, next_ref[...], 0.0)
    n_qk = DN_QK_WIDTH // DN_DK
    for c in range(DN_CONV_DIM // LANES):
        sl = slice(c * LANES, (c + 1) * LANES)
        acc = xe_ref[halo - pad:halo - pad + tt, sl] * cw_ref[0:1, sl]
        for w in range(1, CONV_W):
            acc = acc + xe_ref[halo - pad + w:halo - pad + w + tt, sl] * cw_ref[w:w + 1, sl]
        h = acc * _sigmoid(acc)
        if c < 2 * n_qk:
            h = h * lax.rsqrt(jnp.sum(h * h, axis=-1, keepdims=True) + RMS_EPS)
        if c < n_qk:
            q_ref[:, sl] = h * (DN_DK ** -0.5)
        elif c < 2 * n_qk:
            k_ref[:, (c - n_qk) * LANES:(c - n_qk + 1) * LANES] = h
        else:
            v_ref[:, (c - 2 * n_qk) * LANES:(c - 2 * n_qk + 1) * LANES] = h


def _dnprep(proj3, conv_w):
    b, t, _ = proj3.shape
    tt = min(256, t)
    nt = t // tt
    hb = tt // SUBLANES
    blk = (ATTN_WIDTH + DN_WIDTH) // DN_CONV_DIM
    return pl.pallas_call(
        functools.partial(_dnprep_kernel, tt=tt, nt=nt),
        grid=(b, nt),
        in_specs=[
            pl.BlockSpec((None, tt, DN_CONV_DIM), lambda bi, i: (bi, i, blk)),
            pl.BlockSpec((None, SUBLANES, DN_CONV_DIM), lambda bi, i: (bi, jnp.maximum(i * hb - 1, 0), blk)),
            pl.BlockSpec((None, SUBLANES, DN_CONV_DIM),
                         lambda bi, i: (bi, jnp.minimum((i + 1) * hb, t // SUBLANES - 1), blk)),
            pl.BlockSpec((SUBLANES, DN_CONV_DIM), lambda bi, i: (0, 0)),
        ],
        out_specs=[
            pl.BlockSpec((None, tt, DN_QK_WIDTH), lambda bi, i: (bi, i, 0)),
            pl.BlockSpec((None, tt, DN_QK_WIDTH), lambda bi, i: (bi, i, 0)),
            pl.BlockSpec((None, tt, DN_WIDTH), lambda bi, i: (bi, i, 0)),
        ],
        out_shape=[
            jax.ShapeDtypeStruct((b, t, DN_QK_WIDTH), F32),
            jax.ShapeDtypeStruct((b, t, DN_QK_WIDTH), F32),
            jax.ShapeDtypeStruct((b, t, DN_WIDTH), F32),
        ],
        scratch_shapes=[pltpu.VMEM((tt + 2 * SUBLANES, DN_CONV_DIM), F32)],
        compiler_params=_cparams(("parallel", "arbitrary")),
        name="dnprep",
    )(proj3, proj3, proj3, conv_w)


def _pick_lane(x, idx):
    lane = lax.broadcasted_iota(jnp.int32, x.shape, x.ndim - 1)
    return jnp.sum(jnp.where(lane == idx, x, 0.0), axis=-1, keepdims=True)


def _dn_chunk(q, k, v, g_col, beta_col, s_ref, backward):
    c = DN_CHUNK
    row = lax.broadcasted_iota(jnp.int32, (c, c), 0)
    col = lax.broadcasted_iota(jnp.int32, (c, c), 1)
    if backward:
        incl = col >= row
        strict = col > row
        last = 0
    else:
        incl = col <= row
        strict = col < row
        last = c - 1
    gc = jnp.dot(incl.astype(F32), jnp.broadcast_to(g_col, (c, c)), preferred_element_type=F32,
                 precision=lax.Precision.HIGHEST)
    decay = jnp.exp(jnp.where(incl, gc - gc.T, -jnp.inf))
    eg = jnp.exp(gc)
    g_last = gc[last:last + 1, :]
    kb = k * beta_col
    a = jnp.where(strict, -(_bdot(kb, k, NT_DIMS) * decay), 0.0)
    x = a
    for _ in range(int(math.log2(c)) - 1):
        a = _bdot(a, a)
        x = x + a + _bdot(x, a)
    rhs = jnp.concatenate([v * beta_col, kb * eg], axis=1)
    uw = rhs + _bdot(x, rhs)
    u = uw[:, :DN_DV]
    w = uw[:, DN_DV:]
    a_intra = _bdot(q, k, NT_DIMS) * decay
    q_dec = q * eg
    k_tail = k * jnp.exp(g_last - gc)
    s = s_ref[...]
    r = _bdot(jnp.concatenate([w, q_dec], axis=0), s)
    v_new = u - r[:c]
    o = r[c:] + _bdot(a_intra, v_new)
    s_ref[...] = s * jnp.exp(g_last) + _bdot(k_tail, v_new, TN_DIMS)
    return o


def _dnscan_kernel(qf_ref, kf_ref, vf_ref, gf_ref, qb_ref, kb_ref, vb_ref, gb_ref, par_ref,
                   of_ref, ob_ref, sf_ref, sb_ref, *, nc):
    h = pl.program_id(1)

    @pl.when(pl.program_id(2) == 0)
    def _():
        sf_ref[...] = jnp.zeros(sf_ref.shape, F32)
        sb_ref[...] = jnp.zeros(sb_ref.shape, F32)

    par = par_ref[...]
    for d, (q_ref, k_ref, v_ref, g_ref, o_ref, s_ref) in enumerate(
            ((qf_ref, kf_ref, vf_ref, gf_ref, of_ref, sf_ref), (qb_ref, kb_ref, vb_ref, gb_ref, ob_ref, sb_ref))):
        a_log = _pick_lane(par[d:d + 1, :], h)
        dt_bias = _pick_lane(par[2 + d:3 + d, :], h)
        gates = g_ref[...]
        beta = _sigmoid(_pick_lane(gates, d * DN_V_HEADS + h))
        da = _pick_lane(gates, (2 + d) * DN_V_HEADS + h) + dt_bias
        softplus = jnp.maximum(da, 0.0) + jnp.log(1.0 + jnp.exp(-jnp.abs(da)))
        g = -jnp.exp(a_log) * softplus
        for j in range(nc):
            ci = nc - 1 - j if d else j
            sl = slice(ci * DN_CHUNK, (ci + 1) * DN_CHUNK)
            o_ref[sl, :] = _dn_chunk(q_ref[sl, :], k_ref[sl, :], v_ref[sl, :], g[sl, :], beta[sl, :],
                                     s_ref, backward=bool(d))


def _dnscan(dq, dk, dv, gates3, params):
    b, t, _ = dv.shape
    tt = min(2 * DN_CHUNK, t)
    nt = t // tt
    rep = DN_V_HEADS // DN_QK_HEADS
    fwd_qk = pl.BlockSpec((None, tt, DN_DK), lambda bi, h, i: (bi, i, h // rep))
    bwd_qk = pl.BlockSpec((None, tt, DN_DK), lambda bi, h, i: (bi, nt - 1 - i, h // rep))
    fwd_v = pl.BlockSpec((None, tt, DN_DV), lambda bi, h, i: (bi, i, h))
    bwd_v = pl.BlockSpec((None, tt, DN_DV), lambda bi, h, i: (bi, nt - 1 - i, h))
    fwd_g = pl.BlockSpec((None, tt, TAIL_WIDTH), lambda bi, h, i: (bi, i, 0))
    bwd_g = pl.BlockSpec((None, tt, TAIL_WIDTH), lambda bi, h, i: (bi, nt - 1 - i, 0))
    return pl.pallas_call(
        functools.partial(_dnscan_kernel, nc=tt // DN_CHUNK),
        grid=(b, DN_V_HEADS, nt),
        in_specs=[fwd_qk, fwd_qk, fwd_v, fwd_g, bwd_qk, bwd_qk, bwd_v, bwd_g,
                  pl.BlockSpec((SUBLANES, LANES), lambda bi, h, i: (0, 0))],
        out_specs=[fwd_v, bwd_v],
        out_shape=[jax.ShapeDtypeStruct((b, t, DN_WIDTH), F32)] * 2,
        scratch_shapes=[pltpu.VMEM((DN_DK, DN_DV), F32)] * 2,
        compiler_params=_cparams(("parallel", "parallel", "arbitrary")),
        name="dnscan",
    )(dq, dk, dv, gates3, dq, dk, dv, gates3, params)


def _outproj_kernel(x_ref, g0_ref, b0_ref, oa_ref, of_ref, ob_ref, dz_ref, nw_ref, w_ref, g1_ref, b1_ref,
                    y_ref, yb_ref):
    o = of_ref[...] + ob_ref[...]
    heads = []
    for h in range(DN_V_HEADS):
        sl = slice(h * DN_DV, (h + 1) * DN_DV)
        oh = o[:, sl]
        oh = oh * lax.rsqrt(jnp.mean(oh * oh, axis=-1, keepdims=True) + RMS_EPS) * nw_ref[...]
        z = dz_ref[:, sl]
        heads.append((oh * (z * _sigmoid(z))).astype(BF16))
    o_dn = jnp.concatenate(heads, axis=1)
    mix = jnp.dot(oa_ref[...], w_ref[0:ATTN_WIDTH, :], preferred_element_type=F32)
    mix = mix + jnp.dot(o_dn, w_ref[ATTN_WIDTH:ATTN_WIDTH + DN_WIDTH, :], preferred_element_type=F32)
    x0 = _layer_norm(x_ref[...], g0_ref[...], b0_ref[...])
    y = _layer_norm(DEEPNORM_ALPHA * x0 + mix, g1_ref[...], b1_ref[...])
    y_ref[...] = y
    yb_ref[...] = y.astype(BF16)


def _outproj(x, g0, b0, o_attn, o_f, o_b, proj, norm_w, w_out, g1, b1):
    m, d = x.shape
    tm = min(256, m)
    row = lambda i: (i, 0)
    fixed = lambda i: (0, 0)
    return pl.pallas_call(
        _outproj_kernel,
        grid=(m // tm,),
        in_specs=[
            pl.BlockSpec((tm, d), row),
            pl.BlockSpec((1, d), fixed),
            pl.BlockSpec((1, d), fixed),
            pl.BlockSpec((tm, ATTN_WIDTH), row),
            pl.BlockSpec((tm, DN_WIDTH), row),
            pl.BlockSpec((tm, DN_WIDTH), row),
            pl.BlockSpec((tm, DN_WIDTH), lambda i: (i, ATTN_WIDTH // DN_WIDTH)),
            pl.BlockSpec((1, DN_DV), fixed),
            pl.BlockSpec((ATTN_WIDTH + DN_WIDTH, d), fixed),
            pl.BlockSpec((1, d), fixed),
            pl.BlockSpec((1, d), fixed),
        ],
        out_specs=[pl.BlockSpec((tm, d), row), pl.BlockSpec((tm, d), row)],
        out_shape=[jax.ShapeDtypeStruct((m, d), F32), jax.ShapeDtypeStruct((m, d), BF16)],
        compiler_params=_cparams(("parallel",)),
        name="outproj",
    )(x, g0, b0, o_attn, o_f, o_b, proj, norm_w, w_out, g1, b1)


def _peerq_kernel(x_ref, wq_ref, keys_ref, s_ref, *, n_sets, n_keys, half):
    qry = jnp.dot(x_ref[...], wq_ref[...], preferred_element_type=F32).astype(BF16)
    for hp in range(n_sets):
        s_ref[hp * n_keys:(hp + 1) * n_keys, :] = lax.dot_general(
            keys_ref[hp], qry[:, hp * half:(hp + 1) * half], NT_DIMS, preferred_element_type=F32)


def _peerq(xb, wq, keys):
    m, d = xb.shape
    n_sets, n_keys, half = keys.shape
    tm = min(512, m)
    return pl.pallas_call(
        functools.partial(_peerq_kernel, n_sets=n_sets, n_keys=n_keys, half=half),
        grid=(m // tm,),
        in_specs=[
            pl.BlockSpec((tm, d), lambda i: (i, 0)),
            pl.BlockSpec((d, n_sets * half), lambda i: (0, 0)),
            pl.BlockSpec((n_sets, n_keys, half), lambda i: (0, 0, 0)),
        ],
        out_specs=pl.BlockSpec((n_sets * n_keys, tm), lambda i: (0, i)),
        out_shape=jax.ShapeDtypeStruct((n_sets * n_keys, m), F32),
        compiler_params=_cparams(("parallel",)),
        name="peerq",
    )(xb, wq, keys)


def _extract_top(work, n):
    tops = []
    for _ in range(n):
        mx = jnp.max(work, axis=0, keepdims=True)
        tops.append(mx)
        work = jnp.where(work == mx, -jnp.inf, work)
    return tops


def _peertopk_kernel(s_ref, st_ref, *, n_keys):
    k = PEER_TOPK
    for h in range(PEER_HEADS):
        a = jnp.concatenate(_extract_top(s_ref[(2 * h) * n_keys:(2 * h + 1) * n_keys, :], k), axis=0)
        b = jnp.concatenate(_extract_top(s_ref[(2 * h + 1) * n_keys:(2 * h + 2) * n_keys, :], k), axis=0)
        cand = jnp.concatenate([a + b[j:j + 1, :] for j in range(k)], axis=0)
        sel = _extract_top(cand, k)
        z = jnp.exp(sel[0] - sel[0])
        for r in range(1, k):
            z = z + jnp.exp(sel[r] - sel[0])
        st_ref[h:h + 1, :] = sel[k - 1]
        st_ref[PEER_HEADS + h:PEER_HEADS + h + 1, :] = a[0:1, :]
        st_ref[2 * PEER_HEADS + h:2 * PEER_HEADS + h + 1, :] = b[0:1, :]
        st_ref[3 * PEER_HEADS + h:3 * PEER_HEADS + h + 1, :] = 1.0 / z


def _peertopk(s_t, n_keys):
    rows, m = s_t.shape
    tl = min(256, m)
    return pl.pallas_call(
        functools.partial(_peertopk_kernel, n_keys=n_keys),
        grid=(m // tl,),
        in_specs=[pl.BlockSpec((rows, tl), lambda i: (0, i))],
        out_specs=pl.BlockSpec((4 * PEER_HEADS, tl), lambda i: (0, i)),
        out_shape=jax.ShapeDtypeStruct((4 * PEER_HEADS, m), F32),
        compiler_params=_cparams(("parallel",)),
        name="peertopk",
    )(s_t)


def _peerdense_kernel(xb_ref, u_ref, vt_ref, s_ref, st_ref, o_ref, acc_ref, e1_ref, e2_ref, *, n_keys, eb):
    e = pl.program_id(1)
    nh = PEER_HEADS

    @pl.when(e == 0)
    def _():
        acc_ref[...] = jnp.zeros(acc_ref.shape, F32)
        for h in range(nh):
            s1 = s_ref[(2 * h) * n_keys:(2 * h + 1) * n_keys, :]
            s2 = s_ref[(2 * h + 1) * n_keys:(2 * h + 2) * n_keys, :]
            e1_ref[h] = jnp.exp(s1 - st_ref[nh + h:nh + h + 1, :])
            e2_ref[h] = jnp.exp(s2 - st_ref[2 * nh + h:2 * nh + h + 1, :]) * st_ref[3 * nh + h:3 * nh + h + 1, :]

    pre = lax.dot_general(u_ref[...], xb_ref[...], NT_DIMS, preferred_element_type=F32)
    act = 0.5 * pre * (1.0 + lax.erf(pre * (2.0 ** -0.5)))
    gated = []
    for ii in range(eb // n_keys):
        i = e * (eb // n_keys) + ii
        gate = jnp.zeros((n_keys, act.shape[1]), F32)
        for h in range(nh):
            s1_row = s_ref[pl.ds((2 * h) * n_keys + i, 1), :]
            e1_row = e1_ref[h, pl.ds(i, 1), :]
            pair = s_ref[(2 * h + 1) * n_keys:(2 * h + 2) * n_keys, :] + s1_row
            gate = gate + jnp.where(pair >= st_ref[h:h + 1, :], e2_ref[h] * e1_row, 0.0)
        gated.append((act[ii * n_keys:(ii + 1) * n_keys, :] * gate).astype(BF16))
    acc_ref[...] += jnp.dot(vt_ref[...], jnp.concatenate(gated, axis=0), preferred_element_type=F32)

    @pl.when(e == pl.num_programs(1) - 1)
    def _():
        o_ref[...] = acc_ref[...].T


def _peerdense(xb, exp_u, exp_vt, s_t, stats, n_keys):
    m, d = xb.shape
    n_exp = exp_u.shape[0]
    tm = min(512, m)
    eb = 512
    rows = s_t.shape[0]
    return pl.pallas_call(
        functools.partial(_peerdense_kernel, n_keys=n_keys, eb=eb),
        grid=(m // tm, n_exp // eb),
        in_specs=[
            pl.BlockSpec((tm, d), lambda i, e: (i, 0)),
            pl.BlockSpec((eb, d), lambda i, e: (e, 0)),
            pl.BlockSpec((d, eb), lambda i, e: (0, e)),
            pl.BlockSpec((rows, tm), lambda i, e: (0, i)),
            pl.BlockSpec((4 * PEER_HEADS, tm), lambda i, e: (0, i)),
        ],
        out_specs=pl.BlockSpec((tm, d), lambda i, e: (i, 0)),
        out_shape=jax.ShapeDtypeStruct((m, d), F32),
        scratch_shapes=[
            pltpu.VMEM((d, tm), F32),
            pltpu.VMEM((PEER_HEADS, n_keys, tm), F32),
            pltpu.VMEM((PEER_HEADS, n_keys, tm), F32),
        ],
        compiler_params=_cparams(("parallel", "arbitrary")),
        name="peerdense",
    )(xb, exp_u, exp_vt, s_t, stats)


def _ln2_kernel(x_ref, p_ref, g_ref, b_ref, y_ref):
    y_ref[...] = _layer_norm(DEEPNORM_ALPHA * x_ref[...] + p_ref[...], g_ref[...], b_ref[...])


def _ln2(x, p, g, b):
    m, d = x.shape
    tm = min(512, m)
    row = lambda i: (i, 0)
    fixed = lambda i: (0, 0)
    return pl.pallas_call(
        _ln2_kernel,
        grid=(m // tm,),
        in_specs=[pl.BlockSpec((tm, d), row), pl.BlockSpec((tm, d), row),
                  pl.BlockSpec((1, d), fixed), pl.BlockSpec((1, d), fixed)],
        out_specs=pl.BlockSpec((tm, d), row),
        out_shape=jax.ShapeDtypeStruct((m, d), F32),
        compiler_params=_cparams(("parallel",)),
        name="ln2",
    )(x, p, g, b)


def _prepare_weights(ln_in_g, ln_in_b, w_in, dn_conv_w, dn_a_log, dn_dt_bias, dn_norm_w, w_out, ln1_g, ln1_b,
                     peer_wq, peer_keys, peer_u, peer_v, ln2_g, ln2_b):
    d = w_in.shape[0]
    o_ak = ATTN_WIDTH
    o_av = o_ak + ATTN_KV_WIDTH
    o_dqkv = o_av + ATTN_KV_WIDTH
    o_dz = o_dqkv + DN_CONV_DIM
    o_db = o_dz + DN_WIDTH
    w_main = jnp.concatenate(
        [w_in[:, :o_ak], w_in[:, o_dz:o_db], w_in[:, o_dqkv:o_dz], w_in[:, o_ak:o_av], w_in[:, o_av:o_dqkv]],
        axis=1).astype(BF16)
    n_gate = 4 * DN_V_HEADS
    w_tail = jnp.concatenate([w_in[:, o_db:o_db + n_gate], jnp.zeros((d, TAIL_WIDTH - n_gate), w_in.dtype)],
                             axis=1).astype(BF16)
    conv_w = jnp.concatenate([dn_conv_w, jnp.zeros((SUBLANES - CONV_W, DN_CONV_DIM), dn_conv_w.dtype)], axis=0)
    params = jnp.zeros((SUBLANES, LANES), F32)
    params = params.at[0:2, 0:DN_V_HEADS].set(dn_a_log).at[2:4, 0:DN_V_HEADS].set(dn_dt_bias)
    n_heads, _, n_keys, half = peer_keys.shape
    row = lambda v: v.reshape(1, -1)
    return dict(
        ln_in_g=row(ln_in_g), ln_in_b=row(ln_in_b), w_main=w_main, w_tail=w_tail, conv_w=conv_w, dn_params=params,
        dn_norm_w=row(dn_norm_w), w_out=w_out.astype(BF16), ln1_g=row(ln1_g), ln1_b=row(ln1_b),
        peer_wq=peer_wq.astype(BF16), peer_keys=peer_keys.reshape(n_heads * 2, n_keys, half).astype(BF16),
        peer_u=peer_u.astype(BF16), peer_vt=peer_v.astype(BF16).T, ln2_g=row(ln2_g), ln2_b=row(ln2_b),
        n_keys=n_keys)


def _trunk(x, w, q_norm_w, k_norm_w):
    b, t, d = x.shape
    m = b * t
    xf = x.reshape(m, d)
    proj, gates = _inproj(xf, w["ln_in_g"], w["ln_in_b"], w["w_main"], w["w_tail"])
    q, k, v = _qkprep(proj, t, q_norm_w, k_norm_w)
    o_attn = _attention(q, k, v, b, t)
    dq, dk, dv = _dnprep(proj.reshape(b, t, MAIN_WIDTH), w["conv_w"])
    o_f, o_b = _dnscan(dq, dk, dv, gates.reshape(b, t, TAIL_WIDTH), w["dn_params"])
    x1, x1b = _outproj(xf, w["ln_in_g"], w["ln_in_b"], o_attn, o_f.reshape(m, DN_WIDTH), o_b.reshape(m, DN_WIDTH),
                       proj, w["dn_norm_w"], w["w_out"], w["ln1_g"], w["ln1_b"])
    s_t = _peerq(x1b, w["peer_wq"], w["peer_keys"])
    stats = _peertopk(s_t, w["n_keys"])
    peer = _peerdense(x1b, w["peer_u"], w["peer_vt"], s_t, stats, w["n_keys"])
    y = _ln2(x1, peer, w["ln2_g"], w["ln2_b"])
    return y.reshape(b, t, d)


def kernel(x_prompt, x_sample, ln_in_g, ln_in_b, w_in, q_norm_w, k_norm_w, dn_conv_w, dn_a_log, dn_dt_bias,
           dn_norm_w, w_out, ln1_g, ln1_b, peer_wq, peer_keys, peer_u, peer_v, ln2_g, ln2_b):
    w = _prepare_weights(ln_in_g, ln_in_b, w_in[0], dn_conv_w[0], dn_a_log[0], dn_dt_bias[0], dn_norm_w[0],
                         w_out[0], ln1_g[0], ln1_b[0], peer_wq[0], peer_keys[0], peer_u[0], peer_v[0],
                         ln2_g[0], ln2_b[0])
    return (_trunk(x_prompt, w, q_norm_w[0], k_norm_w[0]), _trunk(x_sample, w, q_norm_w[0], k_norm_w[0]))
```

```python
import functools
import math

import jax
import jax.numpy as jnp
from jax import lax
from jax.experimental import pallas as pl
from jax.experimental.pallas import tpu as pltpu

F32 = jnp.float32
BF16 = jnp.bfloat16

HEAD_DIM = 128
ATTN_HEADS = 8
ATTN_KV_HEADS = 2
ATTN_GROUP = ATTN_HEADS // ATTN_KV_HEADS
ATTN_WIDTH = ATTN_HEADS * HEAD_DIM
ATTN_KV_WIDTH = ATTN_KV_HEADS * HEAD_DIM
DN_QK_HEADS = 4
DN_V_HEADS = 8
DN_DK = 128
DN_DV = 128
DN_QK_WIDTH = DN_QK_HEADS * DN_DK
DN_WIDTH = DN_V_HEADS * DN_DV
DN_CONV_DIM = 2 * DN_QK_WIDTH + DN_WIDTH
GRID_W = 64
ROPE_THETA = 10000.0
CONV_W = 5
PEER_HEADS = 8
PEER_TOPK = 16
DEPTH = 1
DEEPNORM_ALPHA = (2.0 * DEPTH) ** 0.25
LN_EPS = 1e-5
RMS_EPS = 1e-6
LOG2_E = math.log2(math.e)

LANES = 128
SUBLANES = 8
VMEM_LIMIT_BYTES = 56 * 1024 * 1024

MAIN_WIDTH = ATTN_WIDTH + DN_WIDTH + DN_CONV_DIM + 2 * ATTN_KV_WIDTH
TAIL_WIDTH = LANES
PEER_SUB = 256
PEER_MIX_GROUP = 2
V_ONES_ROWS = 16
DN_CHUNK = 128

NT_DIMS = (((1,), (1,)), ((), ()))
TN_DIMS = (((0,), (0,)), ((), ()))


def _cparams(sem):
    return pltpu.CompilerParams(dimension_semantics=sem, vmem_limit_bytes=VMEM_LIMIT_BYTES)


def _layer_norm(x, g, b):
    mu = jnp.mean(x, axis=-1, keepdims=True)
    xc = x - mu
    var = jnp.mean(xc * xc, axis=-1, keepdims=True)
    return xc * lax.rsqrt(var + LN_EPS) * g + b


def _sigmoid(x):
    return 1.0 / (1.0 + jnp.exp(-x))


def _bdot(a, b, dims=None):
    a = a.astype(BF16)
    b = b.astype(BF16)
    if dims is None:
        return jnp.dot(a, b, preferred_element_type=F32)
    return lax.dot_general(a, b, dims, preferred_element_type=F32)


def _inproj_kernel(x_ref, g_ref, b_ref, w_ref, wt_ref, o_ref, ot_ref, xn_ref):
    @pl.when(pl.program_id(1) == 0)
    def _():
        xn = _layer_norm(x_ref[...], g_ref[...], b_ref[...]).astype(BF16)
        xn_ref[...] = xn
        ot_ref[...] = jnp.dot(xn, wt_ref[...], preferred_element_type=F32)

    o_ref[...] = jnp.dot(xn_ref[...], w_ref[...], preferred_element_type=F32)


def _inproj(x, g, b, w_main, w_tail):
    m, d = x.shape
    tm = min(1024, m)
    tn = 512
    return pl.pallas_call(
        _inproj_kernel,
        grid=(m // tm, MAIN_WIDTH // tn),
        in_specs=[
            pl.BlockSpec((tm, d), lambda i, j: (i, 0)),
            pl.BlockSpec((1, d), lambda i, j: (0, 0)),
            pl.BlockSpec((1, d), lambda i, j: (0, 0)),
            pl.BlockSpec((d, tn), lambda i, j: (0, j)),
            pl.BlockSpec((d, TAIL_WIDTH), lambda i, j: (0, 0)),
        ],
        out_specs=[
            pl.BlockSpec((tm, tn), lambda i, j: (i, j)),
            pl.BlockSpec((tm, TAIL_WIDTH), lambda i, j: (i, 0)),
        ],
        out_shape=[
            jax.ShapeDtypeStruct((m, MAIN_WIDTH), F32),
            jax.ShapeDtypeStruct((m, TAIL_WIDTH), F32),
        ],
        scratch_shapes=[pltpu.VMEM((tm, d), BF16)],
        compiler_params=_cparams(("parallel", "arbitrary")),
        name="inproj",
    )(x, g, b, w_main, w_tail)


def _qkprep_kernel(aq_ref, ak_ref, av_ref, cos_ref, sin_ref, qw_ref, kw_ref, q_ref, k_ref, v_ref):
    cos = cos_ref[...]
    sin = sin_ref[...]
    lane = lax.broadcasted_iota(jnp.int32, cos.shape, 1)
    first = (lane % (HEAD_DIM // 2)) < (HEAD_DIM // 4)

    def prep(x, w, scale):
        ms = jnp.mean(x * x, axis=-1, keepdims=True)
        y = x * lax.rsqrt(ms + RMS_EPS) * w
        sw = jnp.where(first, pltpu.roll(y, LANES - HEAD_DIM // 4, 1), pltpu.roll(y, HEAD_DIM // 4, 1))
        r = y * cos + sw * sin
        return r * scale

    for h in range(ATTN_HEADS):
        sl = slice(h * HEAD_DIM, (h + 1) * HEAD_DIM)
        q_ref[h] = prep(aq_ref[:, sl], qw_ref[...], HEAD_DIM ** -0.5 * LOG2_E).T.astype(BF16)
    for h in range(ATTN_KV_HEADS):
        sl = slice(h * HEAD_DIM, (h + 1) * HEAD_DIM)
        k_ref[:, sl] = prep(ak_ref[:, sl], kw_ref[...], 1.0).astype(BF16)
        v_ref[h, 0:HEAD_DIM, :] = av_ref[:, sl].T.astype(BF16)
        v_ref[h, HEAD_DIM:, :] = jnp.ones((V_ONES_ROWS, v_ref.shape[2]), BF16)


def _rope_tables(t):
    half = HEAD_DIM // 4
    inv = ROPE_THETA ** (-jnp.arange(half, dtype=F32) * (2.0 / (HEAD_DIM // 2)))
    pos = jnp.arange(t, dtype=jnp.int32)
    rows = (pos // GRID_W).astype(F32)[:, None] * inv[None, :]
    cols = (pos % GRID_W).astype(F32)[:, None] * inv[None, :]
    cos = jnp.concatenate([jnp.cos(rows), jnp.cos(rows), jnp.cos(cols), jnp.cos(cols)], axis=-1)
    sin = jnp.concatenate([-jnp.sin(rows), jnp.sin(rows), -jnp.sin(cols), jnp.sin(cols)], axis=-1)
    return cos, sin


def _qkprep(proj3, q_norm_w, k_norm_w):
    b, t, _ = proj3.shape
    tm = min(512, t)
    cos, sin = _rope_tables(t)
    kv_blk = (MAIN_WIDTH - 2 * ATTN_KV_WIDTH) // ATTN_KV_WIDTH
    return pl.pallas_call(
        _qkprep_kernel,
        grid=(b, t // tm),
        in_specs=[
            pl.BlockSpec((None, tm, ATTN_WIDTH), lambda bi, i: (bi, i, 0)),
            pl.BlockSpec((None, tm, ATTN_KV_WIDTH), lambda bi, i: (bi, i, kv_blk)),
            pl.BlockSpec((None, tm, ATTN_KV_WIDTH), lambda bi, i: (bi, i, kv_blk + 1)),
            pl.BlockSpec((tm, HEAD_DIM), lambda bi, i: (i, 0)),
            pl.BlockSpec((tm, HEAD_DIM), lambda bi, i: (i, 0)),
            pl.BlockSpec((1, HEAD_DIM), lambda bi, i: (0, 0)),
            pl.BlockSpec((1, HEAD_DIM), lambda bi, i: (0, 0)),
        ],
        out_specs=[
            pl.BlockSpec((None, ATTN_HEADS, HEAD_DIM, tm), lambda bi, i: (bi, 0, 0, i)),
            pl.BlockSpec((None, tm, ATTN_KV_WIDTH), lambda bi, i: (bi, i, 0)),
            pl.BlockSpec((None, ATTN_KV_HEADS, HEAD_DIM + V_ONES_ROWS, tm), lambda bi, i: (bi, 0, 0, i)),
        ],
        out_shape=[
            jax.ShapeDtypeStruct((b, ATTN_HEADS, HEAD_DIM, t), BF16),
            jax.ShapeDtypeStruct((b, t, ATTN_KV_WIDTH), BF16),
            jax.ShapeDtypeStruct((b, ATTN_KV_HEADS, HEAD_DIM + V_ONES_ROWS, t), BF16),
        ],
        compiler_params=_cparams(("parallel", "parallel")),
        name="qkprep",
    )(proj3, proj3, proj3, cos, sin, q_norm_w.reshape(1, HEAD_DIM), k_norm_w.reshape(1, HEAD_DIM))


def _attn_kernel(qt_ref, k_ref, vt_ref, o_ref, m_ref, acc_ref, *, tk, cpb, nk):
    m_ref[...] = jnp.full(m_ref.shape, -jnp.inf, F32)
    acc_ref[...] = jnp.zeros(acc_ref.shape, F32)

    def body(c, carry):
        ks, vts = [], []
        for j in range(cpb):
            off = pl.multiple_of((c * cpb + j) * tk, tk)
            ks.append(k_ref[pl.ds(off, tk), :])
            vts.append(vt_ref[:, pl.ds(off, tk)])

        def scores(j, g):
            return jnp.dot(ks[j], qt_ref[g], preferred_element_type=F32)

        def softmax(g, s):
            m_prev = m_ref[g]
            m_new = jnp.maximum(m_prev, jnp.max(s, axis=0, keepdims=True))
            m_ref[g] = m_new
            return jnp.exp2(m_prev - m_new), jnp.exp2((s - m_new).astype(BF16))

        def values(j, g, alpha, p):
            acc_ref[g] = acc_ref[g] * alpha + jnp.dot(vts[j], p, preferred_element_type=F32)

        tiles = [(j, g) for j in range(cpb) for g in range(ATTN_GROUP)]
        s = {i: scores(*tiles[i]) for i in range(2)}
        ap = {}
        for i, (j, g) in enumerate(tiles):
            ap[i] = softmax(g, s.pop(i))
            if i + 2 < len(tiles):
                s[i + 2] = scores(*tiles[i + 2])
            if i >= 1:
                values(*tiles[i - 1], *ap.pop(i - 1))
        values(*tiles[-1], *ap.pop(len(tiles) - 1))
        return carry

    lax.fori_loop(0, nk // cpb, body, 0)
    for g in range(ATTN_GROUP):
        out = acc_ref[g, 0:HEAD_DIM, :] * (1.0 / acc_ref[g, HEAD_DIM:HEAD_DIM + 1, :])
        o_ref[:, g * HEAD_DIM:(g + 1) * HEAD_DIM] = out.T.astype(o_ref.dtype)


def _attention(qt, k, vt):
    b, _, _, t = qt.shape
    vrows = vt.shape[2]
    tq = min(256, t)
    tk = min(512, t)
    nk = t // tk
    cpb = math.gcd(nk, 4)
    gw = ATTN_GROUP * HEAD_DIM
    out = pl.pallas_call(
        functools.partial(_attn_kernel, tk=tk, cpb=cpb, nk=nk),
        grid=(b, ATTN_KV_HEADS, t // tq),
        in_specs=[
            pl.BlockSpec((None, ATTN_GROUP, HEAD_DIM, tq), lambda bi, kv, qi: (bi, kv, 0, qi)),
            pl.BlockSpec((None, t, HEAD_DIM), lambda bi, kv, qi: (bi, 0, kv)),
            pl.BlockSpec((None, None, vrows, t), lambda bi, kv, qi: (bi, kv, 0, 0)),
        ],
        out_specs=pl.BlockSpec((None, tq, gw), lambda bi, kv, qi: (bi, qi, kv)),
        out_shape=jax.ShapeDtypeStruct((b, t, ATTN_WIDTH), BF16),
        scratch_shapes=[
            pltpu.VMEM((ATTN_GROUP, 1, tq), F32),
            pltpu.VMEM((ATTN_GROUP, vrows, tq), F32),
        ],
        compiler_params=_cparams(("parallel", "parallel", "arbitrary")),
        name="attn",
    )(qt, k, vt)
    return out.reshape(b * t, ATTN_WIDTH)


def _dnprep_kernel(x_ref, prev_ref, next_ref, cw_ref, q_ref, k_ref, v_ref, xe_ref, *, tt, nt):
    i = pl.program_id(1)
    halo = SUBLANES
    pad = CONV_W // 2
    xe_ref[0:halo, :] = jnp.where(i > 0, prev_ref[...], 0.0)
    xe_ref[halo:halo + tt, :] = x_ref[...]
    xe_ref[halo + tt:halo + tt + halo, :] = jnp.where(i < nt - 1, next_ref[...], 0.0)
    n_qk = DN_QK_WIDTH // DN_DK
    for c in range(DN_CONV_DIM // LANES):
        sl = slice(c * LANES, (c + 1) * LANES)
        acc = xe_ref[halo - pad:halo - pad + tt, sl] * cw_ref[0:1, sl]
        for w in range(1, CONV_W):
            acc = acc + xe_ref[halo - pad + w:halo - pad + w + tt, sl] * cw_ref[w:w + 1, sl]
        h = acc * _sigmoid(acc)
        if c < 2 * n_qk:
            h = h * lax.rsqrt(jnp.sum(h * h, axis=-1, keepdims=True) + RMS_EPS)
        if c < n_qk:
            q_ref[:, sl] = h * (DN_DK ** -0.5)
        elif c < 2 * n_qk:
            k_ref[:, (c - n_qk) * LANES:(c - n_qk + 1) * LANES] = h
        else:
            v_ref[:, (c - 2 * n_qk) * LANES:(c - 2 * n_qk + 1) * LANES] = h


def _dnprep(proj3, conv_w):
    b, t, _ = proj3.shape
    tt = min(256, t)
    nt = t // tt
    hb = tt // SUBLANES
    blk = (ATTN_WIDTH + DN_WIDTH) // DN_CONV_DIM
    return pl.pallas_call(
        functools.partial(_dnprep_kernel, tt=tt, nt=nt),
        grid=(b, nt),
        in_specs=[
            pl.BlockSpec((None, tt, DN_CONV_DIM), lambda bi, i: (bi, i, blk)),
            pl.BlockSpec((None, SUBLANES, DN_CONV_DIM), lambda bi, i: (bi, jnp.maximum(i * hb - 1, 0), blk)),
            pl.BlockSpec((None, SUBLANES, DN_CONV_DIM),
                         lambda bi, i: (bi, jnp.minimum((i + 1) * hb, t // SUBLANES - 1), blk)),
            pl.BlockSpec((SUBLANES, DN_CONV_DIM), lambda bi, i: (0, 0)),
        ],
        out_specs=[
            pl.BlockSpec((None, tt, DN_QK_WIDTH), lambda bi, i: (bi, i, 0)),
            pl.BlockSpec((None, tt, DN_QK_WIDTH), lambda bi, i: (bi, i, 0)),
            pl.BlockSpec((None, tt, DN_WIDTH), lambda bi, i: (bi, i, 0)),
        ],
        out_shape=[
            jax.ShapeDtypeStruct((b, t, DN_QK_WIDTH), F32),
            jax.ShapeDtypeStruct((b, t, DN_QK_WIDTH), F32),
            jax.ShapeDtypeStruct((b, t, DN_WIDTH), F32),
        ],
        scratch_shapes=[pltpu.VMEM((tt + 2 * SUBLANES, DN_CONV_DIM), F32)],
        compiler_params=_cparams(("parallel", "arbitrary")),
        name="dnprep",
    )(proj3, proj3, proj3, conv_w)


def _pick_lane(x, idx):
    lane = lax.broadcasted_iota(jnp.int32, x.shape, x.ndim - 1)
    return jnp.sum(jnp.where(lane == idx, x, 0.0), axis=-1, keepdims=True)


def _exact_tri_dot(tri, x):
    hi = x.astype(BF16)
    r1 = x - hi.astype(F32)
    mid = r1.astype(BF16)
    lo = (r1 - mid.astype(F32)).astype(BF16)
    t = tri.astype(BF16)
    out = jnp.dot(t, hi, preferred_element_type=F32)
    out = out + jnp.dot(t, mid, preferred_element_type=F32)
    return out + jnp.dot(t, lo, preferred_element_type=F32)


def _dnscan_kernel(qf_ref, kf_ref, vf_ref, gf_ref, qb_ref, kb_ref, vb_ref, gb_ref, par_ref,
                   of_ref, ob_ref, sf_ref, sb_ref, *, nc):
    h = pl.program_id(1)
    c = DN_CHUNK

    @pl.when(pl.program_id(2) == 0)
    def _():
        sf_ref[...] = jnp.zeros(sf_ref.shape, F32)
        sb_ref[...] = jnp.zeros(sb_ref.shape, F32)

    row = lax.broadcasted_iota(jnp.int32, (c, c), 0)
    col = lax.broadcasted_iota(jnp.int32, (c, c), 1)
    incl = (col <= row, col >= row)
    strict = (col < row, col > row)
    last = (c - 1, 0)
    refs = ((qf_ref, kf_ref, vf_ref, gf_ref, of_ref, sf_ref), (qb_ref, kb_ref, vb_ref, gb_ref, ob_ref, sb_ref))
    par = par_ref[...]

    units = []
    gcs = {}
    betas = {}
    for d, (q_ref, k_ref, v_ref, g_ref, o_ref, s_ref) in enumerate(refs):
        a_log = _pick_lane(par[d:d + 1, :], h)
        dt_bias = _pick_lane(par[2 + d:3 + d, :], h)
        gates = g_ref[...]
        beta = _sigmoid(_pick_lane(gates, d * DN_V_HEADS + h))
        da = _pick_lane(gates, (2 + d) * DN_V_HEADS + h) + dt_bias
        softplus = jnp.maximum(da, 0.0) + jnp.log(1.0 + jnp.exp(-jnp.abs(da)))
        g = -jnp.exp(a_log) * softplus
        g_cols = jnp.zeros((c, c), F32)
        for ci in range(nc):
            g_cols = jnp.where(col == ci, g[ci * c:(ci + 1) * c, :], g_cols)
        gc_all = _exact_tri_dot(incl[d], g_cols)
        for j in range(nc):
            ci = nc - 1 - j if d else j
            units.append((d, ci))
            gcs[d, ci] = jnp.broadcast_to(gc_all[:, ci:ci + 1], (c, c))
            betas[d, ci] = beta[ci * c:(ci + 1) * c, :]

    def rows(ref, u):
        return ref[u[1] * c:(u[1] + 1) * c, :]

    q = {u: rows(refs[u[0]][0], u) for u in units}
    k = {u: rows(refs[u[0]][1], u) for u in units}
    v = {u: rows(refs[u[0]][2], u) for u in units}
    decay = {u: jnp.exp(jnp.where(incl[u[0]], gcs[u] - gcs[u].T, -jnp.inf)) for u in units}
    eg = {u: jnp.exp(gcs[u]) for u in units}
    g_last = {u: gcs[u][last[u[0]]:last[u[0]] + 1, :] for u in units}
    kb = {u: k[u] * betas[u] for u in units}
    a = {u: jnp.where(strict[u[0]], -(_bdot(kb[u], k[u], NT_DIMS) * decay[u]), 0.0) for u in units}
    a_intra = {u: _bdot(q[u], k[u], NT_DIMS) * decay[u] for u in units}
    x = dict(a)
    for _ in range(int(math.log2(c)) - 1):
        a = {u: _bdot(a[u], a[u]) for u in units}
        x = {u: x[u] + a[u] + _bdot(x[u], a[u]) for u in units}
    uw = {}
    for u in units:
        rhs = jnp.concatenate([v[u] * betas[u], kb[u] * eg[u]], axis=1)
        uw[u] = rhs + _bdot(x[u], rhs)
    wq = {u: jnp.concatenate([uw[u][:, DN_DV:], q[u] * eg[u]], axis=0) for u in units}
    k_tail = {u: k[u] * jnp.exp(g_last[u] - gcs[u]) for u in units}
    s_decay = {u: jnp.exp(g_last[u]) for u in units}

    for j in range(nc):
        step = [(0, j), (1, nc - 1 - j)]
        s = {u: refs[u[0]][5][...] for u in step}
        r = {u: _bdot(wq[u], s[u]) for u in step}
        v_new = {u: uw[u][:, :DN_DV] - r[u][:c] for u in step}
        for u in step:
            refs[u[0]][4][u[1] * c:(u[1] + 1) * c, :] = r[u][c:] + _bdot(a_intra[u], v_new[u])
        for u in step:
            refs[u[0]][5][...] = s[u] * s_decay[u] + _bdot(k_tail[u], v_new[u], TN_DIMS)


def _dnscan(dq, dk, dv, gates3, params):
    b, t, _ = dv.shape
    tt = min(4 * DN_CHUNK, t)
    nt = t // tt
    rep = DN_V_HEADS // DN_QK_HEADS
    fwd_qk = pl.BlockSpec((None, tt, DN_DK), lambda bi, h, i: (bi, i, h // rep))
    bwd_qk = pl.BlockSpec((None, tt, DN_DK), lambda bi, h, i: (bi, nt - 1 - i, h // rep))
    fwd_v = pl.BlockSpec((None, tt, DN_DV), lambda bi, h, i: (bi, i, h))
    bwd_v = pl.BlockSpec((None, tt, DN_DV), lambda bi, h, i: (bi, nt - 1 - i, h))
    fwd_g = pl.BlockSpec((None, tt, TAIL_WIDTH), lambda bi, h, i: (bi, i, 0))
    bwd_g = pl.BlockSpec((None, tt, TAIL_WIDTH), lambda bi, h, i: (bi, nt - 1 - i, 0))
    return pl.pallas_call(
        functools.partial(_dnscan_kernel, nc=tt // DN_CHUNK),
        grid=(b, DN_V_HEADS, nt),
        in_specs=[fwd_qk, fwd_qk, fwd_v, fwd_g, bwd_qk, bwd_qk, bwd_v, bwd_g,
                  pl.BlockSpec((SUBLANES, LANES), lambda bi, h, i: (0, 0))],
        out_specs=[fwd_v, bwd_v],
        out_shape=[jax.ShapeDtypeStruct((b, t, DN_WIDTH), F32)] * 2,
        scratch_shapes=[pltpu.VMEM((DN_DK, DN_DV), F32)] * 2,
        compiler_params=_cparams(("parallel", "parallel", "arbitrary")),
        name="dnscan",
    )(dq, dk, dv, gates3, dq, dk, dv, gates3, params)


def _outproj_kernel(x_ref, g0_ref, b0_ref, oa_ref, of_ref, ob_ref, dz_ref, nw_ref, w_ref, g1_ref, b1_ref,
                    y_ref, yb_ref):
    o = of_ref[...] + ob_ref[...]
    heads = []
    for h in range(DN_V_HEADS):
        sl = slice(h * DN_DV, (h + 1) * DN_DV)
        oh = o[:, sl]
        oh = oh * lax.rsqrt(jnp.mean(oh * oh, axis=-1, keepdims=True) + RMS_EPS) * nw_ref[...]
        z = dz_ref[:, sl]
        heads.append((oh * (z * _sigmoid(z))).astype(BF16))
    o_dn = jnp.concatenate(heads, axis=1)
    mix = jnp.dot(oa_ref[...], w_ref[0:ATTN_WIDTH, :], preferred_element_type=F32)
    mix = mix + jnp.dot(o_dn, w_ref[ATTN_WIDTH:ATTN_WIDTH + DN_WIDTH, :], preferred_element_type=F32)
    x0 = _layer_norm(x_ref[...], g0_ref[...], b0_ref[...])
    y = _layer_norm(DEEPNORM_ALPHA * x0 + mix, g1_ref[...], b1_ref[...])
    y_ref[...] = y
    yb_ref[...] = y.astype(BF16)


def _outproj(x, g0, b0, o_attn, o_f, o_b, proj, norm_w, w_out, g1, b1):
    m, d = x.shape
    tm = min(256, m)
    row = lambda i: (i, 0)
    fixed = lambda i: (0, 0)
    return pl.pallas_call(
        _outproj_kernel,
        grid=(m // tm,),
        in_specs=[
            pl.BlockSpec((tm, d), row),
            pl.BlockSpec((1, d), fixed),
            pl.BlockSpec((1, d), fixed),
            pl.BlockSpec((tm, ATTN_WIDTH), row),
            pl.BlockSpec((tm, DN_WIDTH), row),
            pl.BlockSpec((tm, DN_WIDTH), row),
            pl.BlockSpec((tm, DN_WIDTH), lambda i: (i, ATTN_WIDTH // DN_WIDTH)),
            pl.BlockSpec((1, DN_DV), fixed),
            pl.BlockSpec((ATTN_WIDTH + DN_WIDTH, d), fixed),
            pl.BlockSpec((1, d), fixed),
            pl.BlockSpec((1, d), fixed),
        ],
        out_specs=[pl.BlockSpec((tm, d), row), pl.BlockSpec((tm, d), row)],
        out_shape=[jax.ShapeDtypeStruct((m, d), F32), jax.ShapeDtypeStruct((m, d), BF16)],
        compiler_params=_cparams(("parallel",)),
        name="outproj",
    )(x, g0, b0, o_attn, o_f, o_b, proj, norm_w, w_out, g1, b1)


def _peerq_kernel(x_ref, wq_ref, keys_ref, s_ref, *, n_sets, n_keys, half):
    qry = jnp.dot(x_ref[...], wq_ref[...], preferred_element_type=F32).astype(BF16)
    for hp in range(n_sets):
        s_ref[hp * n_keys:(hp + 1) * n_keys, :] = lax.dot_general(
            keys_ref[hp], qry[:, hp * half:(hp + 1) * half], NT_DIMS, preferred_element_type=F32)


def _peerq(xb, wq, keys):
    m, d = xb.shape
    n_sets, n_keys, half = keys.shape
    tm = min(512, m)
    return pl.pallas_call(
        functools.partial(_peerq_kernel, n_sets=n_sets, n_keys=n_keys, half=half),
        grid=(m // tm,),
        in_specs=[
            pl.BlockSpec((tm, d), lambda i: (i, 0)),
            pl.BlockSpec((d, n_sets * half), lambda i: (0, 0)),
            pl.BlockSpec((n_sets, n_keys, half), lambda i: (0, 0, 0)),
        ],
        out_specs=pl.BlockSpec((n_sets * n_keys, tm), lambda i: (0, i)),
        out_shape=jax.ShapeDtypeStruct((n_sets * n_keys, m), F32),
        compiler_params=_cparams(("parallel",)),
        name="peerq",
    )(xb, wq, keys)


def _extract_top(work, n):
    tops = []
    for _ in range(n):
        mx = jnp.max(work, axis=0, keepdims=True)
        tops.append(mx)
        work = jnp.where(work == mx, -jnp.inf, work)
    return tops


def _peertopk_kernel(s_ref, aux_ref):
    k = PEER_TOPK
    for h in range(PEER_HEADS):
        s1 = s_ref[h, 0]
        s2 = s_ref[h, 1]
        a = _extract_top(s1, k)
        b = _extract_top(s2, k)
        a_all = jnp.concatenate(a, axis=0)
        cand = jnp.concatenate([a_all + b[j] for j in range(k)], axis=0)
        sel = _extract_top(cand, k)
        tau = sel[k - 1]
        z = jnp.exp(sel[0] - sel[0])
        for r in range(1, k):
            z = z + jnp.exp(sel[r] - sel[0])
        th = jnp.full(s1.shape, jnp.inf, F32)
        for q in range(k):
            th = jnp.where(s1 + b[q] >= tau, b[q], th)
        aux_ref[0, h] = th
        aux_ref[1, h] = jnp.exp(s1 - a[0])
        aux_ref[2, h] = jnp.exp(s2 - b[0]) * (1.0 / z)


def _peertopk(s4):
    nh, _, n_keys, m = s4.shape
    tl = min(256, m)
    return pl.pallas_call(
        _peertopk_kernel,
        grid=(m // tl,),
        in_specs=[pl.BlockSpec((nh, 2, n_keys, tl), lambda i: (0, 0, 0, i))],
        out_specs=pl.BlockSpec((3, nh, n_keys, tl), lambda i: (0, 0, 0, i)),
        out_shape=jax.ShapeDtypeStruct((3, nh, n_keys, m), F32),
        compiler_params=_cparams(("parallel",)),
        name="peertopk",
    )(s4)


def _peerdense_kernel(xb_ref, u_ref, vt_ref, s2_ref, aux_ref, o_ref, acc_ref, a_ref, *, n_keys, eb):
    e = pl.program_id(1)
    nh = PEER_HEADS

    @pl.when(e == 0)
    def _():
        acc_ref[...] = jnp.zeros(acc_ref.shape, F32)

    sub = PEER_SUB
    n_sub = eb // sub
    xb = xb_ref[...]

    def pre_act(j):
        return lax.dot_general(u_ref[j * sub:(j + 1) * sub, :], xb, NT_DIMS, preferred_element_type=F32)

    def gated_act(j, pre):
        for ii in range(sub // n_keys):
            i = (e * n_sub + j) * (sub // n_keys) + ii
            rows = slice(j * sub + ii * n_keys, j * sub + (ii + 1) * n_keys)
            th_rows = [aux_ref[0, h, pl.ds(i, 1), :] for h in range(nh)]
            e1_rows = [aux_ref[1, h, pl.ds(i, 1), :] for h in range(nh)]
            for tg in range(pre.shape[1] // LANES):
                lanes = slice(tg * LANES, (tg + 1) * LANES)
                gate = jnp.zeros((n_keys, LANES), F32)
                for h in range(nh):
                    sel = s2_ref[h, :, lanes] >= th_rows[h][:, lanes]
                    gate = gate + jnp.where(sel, aux_ref[2, h, :, lanes] * e1_rows[h][:, lanes], 0.0)
                p = pre[ii * n_keys:(ii + 1) * n_keys, lanes]
                act = 0.5 * p * (1.0 + lax.erf(p * (2.0 ** -0.5)))
                a_ref[rows, lanes] = (act * gate).astype(BF16)

    def mix(j0, n):
        cols = slice(j0 * sub, (j0 + n) * sub)
        acc_ref[...] += jnp.dot(vt_ref[:, cols], a_ref[cols, :], preferred_element_type=F32)

    pre = {0: pre_act(0)}
    if n_sub > 1:
        pre[1] = pre_act(1)
    pending = 0
    for j in range(n_sub):
        gated_act(j, pre.pop(j))
        pending += 1
        if j + 2 < n_sub:
            pre[j + 2] = pre_act(j + 2)
        if pending == PEER_MIX_GROUP or j == n_sub - 1:
            mix(j + 1 - pending, pending)
            pending = 0

    @pl.when(e == pl.num_programs(1) - 1)
    def _():
        o_ref[...] = acc_ref[...].T


def _peerdense(xb, exp_u, exp_vt, s4, aux):
    m, d = xb.shape
    n_exp = exp_u.shape[0]
    nh, _, n_keys, _ = s4.shape
    tm = min(512, m)
    eb = 1024
    return pl.pallas_call(
        functools.partial(_peerdense_kernel, n_keys=n_keys, eb=eb),
        grid=(m // tm, n_exp // eb),
        in_specs=[
            pl.BlockSpec((tm, d), lambda i, e: (i, 0)),
            pl.BlockSpec((eb, d), lambda i, e: (e, 0)),
            pl.BlockSpec((d, eb), lambda i, e: (0, e)),
            pl.BlockSpec((nh, None, n_keys, tm), lambda i, e: (0, 1, 0, i)),
            pl.BlockSpec((3, nh, n_keys, tm), lambda i, e: (0, 0, 0, i)),
        ],
        out_specs=pl.BlockSpec((tm, d), lambda i, e: (i, 0)),
        out_shape=jax.ShapeDtypeStruct((m, d), F32),
        scratch_shapes=[pltpu.VMEM((d, tm), F32), pltpu.VMEM((eb, tm), BF16)],
        compiler_params=_cparams(("parallel", "arbitrary")),
        name="peerdense",
    )(xb, exp_u, exp_vt, s4, aux)


def _ln2_kernel(x_ref, p_ref, g_ref, b_ref, y_ref):
    y_ref[...] = _layer_norm(DEEPNORM_ALPHA * x_ref[...] + p_ref[...], g_ref[...], b_ref[...])


def _ln2(x, p, g, b):
    m, d = x.shape
    tm = min(512, m)
    row = lambda i: (i, 0)
    fixed = lambda i: (0, 0)
    return pl.pallas_call(
        _ln2_kernel,
        grid=(m // tm,),
        in_specs=[pl.BlockSpec((tm, d), row), pl.BlockSpec((tm, d), row),
                  pl.BlockSpec((1, d), fixed), pl.BlockSpec((1, d), fixed)],
        out_specs=pl.BlockSpec((tm, d), row),
        out_shape=jax.ShapeDtypeStruct((m, d), F32),
        compiler_params=_cparams(("parallel",)),
        name="ln2",
    )(x, p, g, b)


def _prepare_weights(ln_in_g, ln_in_b, w_in, dn_conv_w, dn_a_log, dn_dt_bias, dn_norm_w, w_out, ln1_g, ln1_b,
                     peer_wq, peer_keys, peer_u, peer_v, ln2_g, ln2_b):
    d = w_in.shape[0]
    o_ak = ATTN_WIDTH
    o_av = o_ak + ATTN_KV_WIDTH
    o_dqkv = o_av + ATTN_KV_WIDTH
    o_dz = o_dqkv + DN_CONV_DIM
    o_db = o_dz + DN_WIDTH
    w_main = jnp.concatenate(
        [w_in[:, :o_ak], w_in[:, o_dz:o_db], w_in[:, o_dqkv:o_dz], w_in[:, o_ak:o_av], w_in[:, o_av:o_dqkv]],
        axis=1).astype(BF16)
    n_gate = 4 * DN_V_HEADS
    w_tail = jnp.concatenate([w_in[:, o_db:o_db + n_gate], jnp.zeros((d, TAIL_WIDTH - n_gate), w_in.dtype)],
                             axis=1).astype(BF16)
    conv_w = jnp.concatenate([dn_conv_w, jnp.zeros((SUBLANES - CONV_W, DN_CONV_DIM), dn_conv_w.dtype)], axis=0)
    params = jnp.zeros((SUBLANES, LANES), F32)
    params = params.at[0:2, 0:DN_V_HEADS].set(dn_a_log).at[2:4, 0:DN_V_HEADS].set(dn_dt_bias)
    n_heads, _, n_keys, half = peer_keys.shape
    row = lambda v: v.reshape(1, -1)
    return dict(
        ln_in_g=row(ln_in_g), ln_in_b=row(ln_in_b), w_main=w_main, w_tail=w_tail, conv_w=conv_w, dn_params=params,
        dn_norm_w=row(dn_norm_w), w_out=w_out.astype(BF16), ln1_g=row(ln1_g), ln1_b=row(ln1_b),
        peer_wq=peer_wq.astype(BF16), peer_keys=peer_keys.reshape(n_heads * 2, n_keys, half).astype(BF16),
        peer_u=peer_u.astype(BF16), peer_vt=peer_v.astype(BF16).T, ln2_g=row(ln2_g), ln2_b=row(ln2_b))


def _trunk(x, w, q_norm_w, k_norm_w):
    b, t, d = x.shape
    m = b * t
    xf = x.reshape(m, d)
    proj, gates = _inproj(xf, w["ln_in_g"], w["ln_in_b"], w["w_main"], w["w_tail"])
    proj3 = proj.reshape(b, t, MAIN_WIDTH)
    o_attn = _attention(*_qkprep(proj3, q_norm_w, k_norm_w))
    dq, dk, dv = _dnprep(proj3, w["conv_w"])
    o_f, o_b = _dnscan(dq, dk, dv, gates.reshape(b, t, TAIL_WIDTH), w["dn_params"])
    x1, x1b = _outproj(xf, w["ln_in_g"], w["ln_in_b"], o_attn, o_f.reshape(m, DN_WIDTH), o_b.reshape(m, DN_WIDTH),
                       proj, w["dn_norm_w"], w["w_out"], w["ln1_g"], w["ln1_b"])
    n_sets, n_keys, _ = w["peer_keys"].shape
    s4 = _peerq(x1b, w["peer_wq"], w["peer_keys"]).reshape(n_sets // 2, 2, n_keys, m)
    peer = _peerdense(x1b, w["peer_u"], w["peer_vt"], s4, _peertopk(s4))
    y = _ln2(x1, peer, w["ln2_g"], w["ln2_b"])
    return y.reshape(b, t, d)


def kernel(x_prompt, x_sample, ln_in_g, ln_in_b, w_in, q_norm_w, k_norm_w, dn_conv_w, dn_a_log, dn_dt_bias,
           dn_norm_w, w_out, ln1_g, ln1_b, peer_wq, peer_keys, peer_u, peer_v, ln2_g, ln2_b):
    w = _prepare_weights(ln_in_g, ln_in_b, w_in[0], dn_conv_w[0], dn_a_log[0], dn_dt_bias[0], dn_norm_w[0],
                         w_out[0], ln1_g[0], ln1_b[0], peer_wq[0], peer_keys[0], peer_u[0], peer_v[0],
                         ln2_g[0], ln2_b[0])
    return (_trunk(x_prompt, w, q_norm_w[0], k_norm_w[0]), _trunk(x_sample, w, q_norm_w[0], k_norm_w[0]))
```

```python
import functools
import math

import jax
import jax.numpy as jnp
from jax import lax
from jax.experimental import pallas as pl
from jax.experimental.pallas import tpu as pltpu

F32 = jnp.float32
BF16 = jnp.bfloat16

HEAD_DIM = 128
ATTN_HEADS = 8
ATTN_KV_HEADS = 2
ATTN_GROUP = ATTN_HEADS // ATTN_KV_HEADS
ATTN_WIDTH = ATTN_HEADS * HEAD_DIM
ATTN_KV_WIDTH = ATTN_KV_HEADS * HEAD_DIM
DN_QK_HEADS = 4
DN_V_HEADS = 8
DN_DK = 128
DN_DV = 128
DN_QK_WIDTH = DN_QK_HEADS * DN_DK
DN_WIDTH = DN_V_HEADS * DN_DV
DN_CONV_DIM = 2 * DN_QK_WIDTH + DN_WIDTH
GRID_W = 64
ROPE_THETA = 10000.0
CONV_W = 5
PEER_HEADS = 8
PEER_TOPK = 16
DEPTH = 1
DEEPNORM_ALPHA = (2.0 * DEPTH) ** 0.25
LN_EPS = 1e-5
RMS_EPS = 1e-6
LOG2_E = math.log2(math.e)

LANES = 128
SUBLANES = 8
VMEM_LIMIT_BYTES = 56 * 1024 * 1024

MAIN_WIDTH = ATTN_WIDTH + DN_WIDTH + DN_CONV_DIM + 2 * ATTN_KV_WIDTH
TAIL_WIDTH = LANES
PEER_SUB = 256
PEER_NO_RANK = 64.0
PEER_TILE_ROWS = 16
V_ONES_ROWS = 16
DN_CHUNK = 128

NT_DIMS = (((1,), (1,)), ((), ()))
TN_DIMS = (((0,), (0,)), ((), ()))


def _cparams(sem):
    return pltpu.CompilerParams(dimension_semantics=sem, vmem_limit_bytes=VMEM_LIMIT_BYTES)


def _layer_norm(x, g, b):
    mu = jnp.mean(x, axis=-1, keepdims=True)
    xc = x - mu
    var = jnp.mean(xc * xc, axis=-1, keepdims=True)
    return xc * lax.rsqrt(var + LN_EPS) * g + b


def _sigmoid(x):
    return 1.0 / (1.0 + jnp.exp(-x))


def _bdot(a, b, dims=None):
    a = a.astype(BF16)
    b = b.astype(BF16)
    if dims is None:
        return jnp.dot(a, b, preferred_element_type=F32)
    return lax.dot_general(a, b, dims, preferred_element_type=F32)


def _inproj_kernel(x_ref, g_ref, b_ref, w_ref, wt_ref, o_ref, ot_ref, xn_ref):
    @pl.when(pl.program_id(1) == 0)
    def _():
        xn = _layer_norm(x_ref[...], g_ref[...], b_ref[...]).astype(BF16)
        xn_ref[...] = xn
        ot_ref[...] = jnp.dot(xn, wt_ref[...], preferred_element_type=F32)

    o_ref[...] = jnp.dot(xn_ref[...], w_ref[...], preferred_element_type=F32)


def _inproj(x, g, b, w_main, w_tail):
    m, d = x.shape
    tm = min(1024, m)
    tn = 512
    return pl.pallas_call(
        _inproj_kernel,
        grid=(m // tm, MAIN_WIDTH // tn),
        in_specs=[
            pl.BlockSpec((tm, d), lambda i, j: (i, 0)),
            pl.BlockSpec((1, d), lambda i, j: (0, 0)),
            pl.BlockSpec((1, d), lambda i, j: (0, 0)),
            pl.BlockSpec((d, tn), lambda i, j: (0, j)),
            pl.BlockSpec((d, TAIL_WIDTH), lambda i, j: (0, 0)),
        ],
        out_specs=[
            pl.BlockSpec((tm, tn), lambda i, j: (i, j)),
            pl.BlockSpec((tm, TAIL_WIDTH), lambda i, j: (i, 0)),
        ],
        out_shape=[
            jax.ShapeDtypeStruct((m, MAIN_WIDTH), F32),
            jax.ShapeDtypeStruct((m, TAIL_WIDTH), F32),
        ],
        scratch_shapes=[pltpu.VMEM((tm, d), BF16)],
        compiler_params=_cparams(("parallel", "arbitrary")),
        name="inproj",
    )(x, g, b, w_main, w_tail)


def _qkprep_kernel(aq_ref, ak_ref, av_ref, cos_ref, sin_ref, qw_ref, kw_ref, q_ref, k_ref, v_ref):
    cos = cos_ref[...]
    sin = sin_ref[...]
    lane = lax.broadcasted_iota(jnp.int32, cos.shape, 1)
    first = (lane % (HEAD_DIM // 2)) < (HEAD_DIM // 4)

    def prep(x, w, scale):
        ms = jnp.mean(x * x, axis=-1, keepdims=True)
        y = x * lax.rsqrt(ms + RMS_EPS) * w
        sw = jnp.where(first, pltpu.roll(y, LANES - HEAD_DIM // 4, 1), pltpu.roll(y, HEAD_DIM // 4, 1))
        r = y * cos + sw * sin
        return r * scale

    for h in range(ATTN_HEADS):
        sl = slice(h * HEAD_DIM, (h + 1) * HEAD_DIM)
        q_ref[h] = prep(aq_ref[:, sl], qw_ref[...], HEAD_DIM ** -0.5 * LOG2_E).T.astype(BF16)
    for h in range(ATTN_KV_HEADS):
        sl = slice(h * HEAD_DIM, (h + 1) * HEAD_DIM)
        k_ref[:, sl] = prep(ak_ref[:, sl], kw_ref[...], 1.0).astype(BF16)
        v_ref[h, 0:HEAD_DIM, :] = av_ref[:, sl].T.astype(BF16)
        v_ref[h, HEAD_DIM:, :] = jnp.ones((V_ONES_ROWS, v_ref.shape[2]), BF16)


def _rope_tables(t):
    half = HEAD_DIM // 4
    inv = ROPE_THETA ** (-jnp.arange(half, dtype=F32) * (2.0 / (HEAD_DIM // 2)))
    pos = jnp.arange(t, dtype=jnp.int32)
    rows = (pos // GRID_W).astype(F32)[:, None] * inv[None, :]
    cols = (pos % GRID_W).astype(F32)[:, None] * inv[None, :]
    cos = jnp.concatenate([jnp.cos(rows), jnp.cos(rows), jnp.cos(cols), jnp.cos(cols)], axis=-1)
    sin = jnp.concatenate([-jnp.sin(rows), jnp.sin(rows), -jnp.sin(cols), jnp.sin(cols)], axis=-1)
    return cos, sin


def _qkprep(proj3, q_norm_w, k_norm_w):
    b, t, _ = proj3.shape
    tm = min(512, t)
    cos, sin = _rope_tables(t)
    kv_blk = (MAIN_WIDTH - 2 * ATTN_KV_WIDTH) // ATTN_KV_WIDTH
    return pl.pallas_call(
        _qkprep_kernel,
        grid=(b, t // tm),
        in_specs=[
            pl.BlockSpec((None, tm, ATTN_WIDTH), lambda bi, i: (bi, i, 0)),
            pl.BlockSpec((None, tm, ATTN_KV_WIDTH), lambda bi, i: (bi, i, kv_blk)),
            pl.BlockSpec((None, tm, ATTN_KV_WIDTH), lambda bi, i: (bi, i, kv_blk + 1)),
            pl.BlockSpec((tm, HEAD_DIM), lambda bi, i: (i, 0)),
            pl.BlockSpec((tm, HEAD_DIM), lambda bi, i: (i, 0)),
            pl.BlockSpec((1, HEAD_DIM), lambda bi, i: (0, 0)),
            pl.BlockSpec((1, HEAD_DIM), lambda bi, i: (0, 0)),
        ],
        out_specs=[
            pl.BlockSpec((None, ATTN_HEADS, HEAD_DIM, tm), lambda bi, i: (bi, 0, 0, i)),
            pl.BlockSpec((None, tm, ATTN_KV_WIDTH), lambda bi, i: (bi, i, 0)),
            pl.BlockSpec((None, ATTN_KV_HEADS, HEAD_DIM + V_ONES_ROWS, tm), lambda bi, i: (bi, 0, 0, i)),
        ],
        out_shape=[
            jax.ShapeDtypeStruct((b, ATTN_HEADS, HEAD_DIM, t), BF16),
            jax.ShapeDtypeStruct((b, t, ATTN_KV_WIDTH), BF16),
            jax.ShapeDtypeStruct((b, ATTN_KV_HEADS, HEAD_DIM + V_ONES_ROWS, t), BF16),
        ],
        compiler_params=_cparams(("parallel", "parallel")),
        name="qkprep",
    )(proj3, proj3, proj3, cos, sin, q_norm_w.reshape(1, HEAD_DIM), k_norm_w.reshape(1, HEAD_DIM))


def _attn_kernel(qt_ref, k_ref, vt_ref, o_ref, m_ref, acc_ref, *, tk, cpb, nk):
    m_ref[...] = jnp.full(m_ref.shape, -jnp.inf, F32)
    acc_ref[...] = jnp.zeros(acc_ref.shape, F32)

    def body(c, carry):
        ks, vts = [], []
        for j in range(cpb):
            off = pl.multiple_of((c * cpb + j) * tk, tk)
            ks.append(k_ref[pl.ds(off, tk), :])
            vts.append(vt_ref[:, pl.ds(off, tk)])

        def scores(j, g):
            return jnp.dot(ks[j], qt_ref[g], preferred_element_type=F32)

        def softmax(g, s):
            m_prev = m_ref[g]
            m_new = jnp.maximum(m_prev, jnp.max(s, axis=0, keepdims=True))
            m_ref[g] = m_new
            return jnp.exp2(m_prev - m_new), jnp.exp2((s - m_new).astype(BF16))

        def values(j, g, alpha, p):
            acc_ref[g] = acc_ref[g] * alpha + jnp.dot(vts[j], p, preferred_element_type=F32)

        tiles = [(j, g) for j in range(cpb) for g in range(ATTN_GROUP)]
        s = {i: scores(*tiles[i]) for i in range(2)}
        ap = {}
        for i, (j, g) in enumerate(tiles):
            ap[i] = softmax(g, s.pop(i))
            if i + 2 < len(tiles):
                s[i + 2] = scores(*tiles[i + 2])
            if i >= 1:
                values(*tiles[i - 1], *ap.pop(i - 1))
        values(*tiles[-1], *ap.pop(len(tiles) - 1))
        return carry

    lax.fori_loop(0, nk // cpb, body, 0)
    for g in range(ATTN_GROUP):
        out = acc_ref[g, 0:HEAD_DIM, :] * (1.0 / acc_ref[g, HEAD_DIM:HEAD_DIM + 1, :])
        o_ref[:, g * HEAD_DIM:(g + 1) * HEAD_DIM] = out.T.astype(o_ref.dtype)


def _attention(qt, k, vt):
    b, _, _, t = qt.shape
    vrows = vt.shape[2]
    tq = min(256, t)
    tk = min(512, t)
    nk = t // tk
    cpb = math.gcd(nk, 4)
    gw = ATTN_GROUP * HEAD_DIM
    out = pl.pallas_call(
        functools.partial(_attn_kernel, tk=tk, cpb=cpb, nk=nk),
        grid=(b, ATTN_KV_HEADS, t // tq),
        in_specs=[
            pl.BlockSpec((None, ATTN_GROUP, HEAD_DIM, tq), lambda bi, kv, qi: (bi, kv, 0, qi)),
            pl.BlockSpec((None, t, HEAD_DIM), lambda bi, kv, qi: (bi, 0, kv)),
            pl.BlockSpec((None, None, vrows, t), lambda bi, kv, qi: (bi, kv, 0, 0)),
        ],
        out_specs=pl.BlockSpec((None, tq, gw), lambda bi, kv, qi: (bi, qi, kv)),
        out_shape=jax.ShapeDtypeStruct((b, t, ATTN_WIDTH), BF16),
        scratch_shapes=[
            pltpu.VMEM((ATTN_GROUP, 1, tq), F32),
            pltpu.VMEM((ATTN_GROUP, vrows, tq), F32),
        ],
        compiler_params=_cparams(("parallel", "parallel", "arbitrary")),
        name="attn",
    )(qt, k, vt)
    return out.reshape(b * t, ATTN_WIDTH)


def _dnprep_kernel(x_ref, prev_ref, next_ref, cw_ref, q_ref, k_ref, v_ref, xe_ref, *, tt, nt):
    i = pl.program_id(1)
    halo = SUBLANES
    pad = CONV_W // 2
    xe_ref[0:halo, :] = jnp.where(i > 0, prev_ref[...], 0.0)
    xe_ref[halo:halo + tt, :] = x_ref[...]
    xe_ref[halo + tt:halo + tt + halo, :] = jnp.where(i < nt - 1, next_ref[...], 0.0)
    n_qk = DN_QK_WIDTH // DN_DK
    for c in range(DN_CONV_DIM // LANES):
        sl = slice(c * LANES, (c + 1) * LANES)
        acc = xe_ref[halo - pad:halo - pad + tt, sl] * cw_ref[0:1, sl]
        for w in range(1, CONV_W):
            acc = acc + xe_ref[halo - pad + w:halo - pad + w + tt, sl] * cw_ref[w:w + 1, sl]
        h = acc * _sigmoid(acc)
        if c < 2 * n_qk:
            h = h * lax.rsqrt(jnp.sum(h * h, axis=-1, keepdims=True) + RMS_EPS)
        if c < n_qk:
            q_ref[:, sl] = h * (DN_DK ** -0.5)
        elif c < 2 * n_qk:
            k_ref[:, (c - n_qk) * LANES:(c - n_qk + 1) * LANES] = h
        else:
            v_ref[:, (c - 2 * n_qk) * LANES:(c - 2 * n_qk + 1) * LANES] = h


def _dnprep(proj3, conv_w):
    b, t, _ = proj3.shape
    tt = min(256, t)
    nt = t // tt
    hb = tt // SUBLANES
    blk = (ATTN_WIDTH + DN_WIDTH) // DN_CONV_DIM
    return pl.pallas_call(
        functools.partial(_dnprep_kernel, tt=tt, nt=nt),
        grid=(b, nt),
        in_specs=[
            pl.BlockSpec((None, tt, DN_CONV_DIM), lambda bi, i: (bi, i, blk)),
            pl.BlockSpec((None, SUBLANES, DN_CONV_DIM), lambda bi, i: (bi, jnp.maximum(i * hb - 1, 0), blk)),
            pl.BlockSpec((None, SUBLANES, DN_CONV_DIM),
                         lambda bi, i: (bi, jnp.minimum((i + 1) * hb, t // SUBLANES - 1), blk)),
            pl.BlockSpec((SUBLANES, DN_CONV_DIM), lambda bi, i: (0, 0)),
        ],
        out_specs=[
            pl.BlockSpec((None, tt, DN_QK_WIDTH), lambda bi, i: (bi, i, 0)),
            pl.BlockSpec((None, tt, DN_QK_WIDTH), lambda bi, i: (bi, i, 0)),
            pl.BlockSpec((None, tt, DN_WIDTH), lambda bi, i: (bi, i, 0)),
        ],
        out_shape=[
            jax.ShapeDtypeStruct((b, t, DN_QK_WIDTH), F32),
            jax.ShapeDtypeStruct((b, t, DN_QK_WIDTH), F32),
            jax.ShapeDtypeStruct((b, t, DN_WIDTH), F32),
        ],
        scratch_shapes=[pltpu.VMEM((tt + 2 * SUBLANES, DN_CONV_DIM), F32)],
        compiler_params=_cparams(("parallel", "arbitrary")),
        name="dnprep",
    )(proj3, proj3, proj3, conv_w)


def _pick_lane(x, idx):
    lane = lax.broadcasted_iota(jnp.int32, x.shape, x.ndim - 1)
    return jnp.sum(jnp.where(lane == idx, x, 0.0), axis=-1, keepdims=True)


def _exact_tri_dot(tri, x):
    hi = x.astype(BF16)
    r1 = x - hi.astype(F32)
    mid = r1.astype(BF16)
    lo = (r1 - mid.astype(F32)).astype(BF16)
    t = tri.astype(BF16)
    out = jnp.dot(t, hi, preferred_element_type=F32)
    out = out + jnp.dot(t, mid, preferred_element_type=F32)
    return out + jnp.dot(t, lo, preferred_element_type=F32)


def _dnscan_kernel(qf_ref, kf_ref, vf_ref, gf_ref, qb_ref, kb_ref, vb_ref, gb_ref, par_ref,
                   of_ref, ob_ref, sf_ref, sb_ref, *, nc):
    h = pl.program_id(1)
    c = DN_CHUNK

    @pl.when(pl.program_id(2) == 0)
    def _():
        sf_ref[...] = jnp.zeros(sf_ref.shape, F32)
        sb_ref[...] = jnp.zeros(sb_ref.shape, F32)

    row = lax.broadcasted_iota(jnp.int32, (c, c), 0)
    col = lax.broadcasted_iota(jnp.int32, (c, c), 1)
    incl = (col <= row, col >= row)
    strict = (col < row, col > row)
    last = (c - 1, 0)
    refs = ((qf_ref, kf_ref, vf_ref, gf_ref, of_ref, sf_ref), (qb_ref, kb_ref, vb_ref, gb_ref, ob_ref, sb_ref))
    par = par_ref[...]

    units = []
    gcs = {}
    betas = {}
    for d, (q_ref, k_ref, v_ref, g_ref, o_ref, s_ref) in enumerate(refs):
        a_log = _pick_lane(par[d:d + 1, :], h)
        dt_bias = _pick_lane(par[2 + d:3 + d, :], h)
        gates = g_ref[...]
        beta = _sigmoid(_pick_lane(gates, d * DN_V_HEADS + h))
        da = _pick_lane(gates, (2 + d) * DN_V_HEADS + h) + dt_bias
        softplus = jnp.maximum(da, 0.0) + jnp.log(1.0 + jnp.exp(-jnp.abs(da)))
        g = -jnp.exp(a_log) * softplus
        g_cols = jnp.zeros((c, c), F32)
        for ci in range(nc):
            g_cols = jnp.where(col == ci, g[ci * c:(ci + 1) * c, :], g_cols)
        gc_all = _exact_tri_dot(incl[d], g_cols)
        for j in range(nc):
            ci = nc - 1 - j if d else j
            units.append((d, ci))
            gcs[d, ci] = jnp.broadcast_to(gc_all[:, ci:ci + 1], (c, c))
            betas[d, ci] = beta[ci * c:(ci + 1) * c, :]

    def rows(ref, u):
        return ref[u[1] * c:(u[1] + 1) * c, :]

    q = {u: rows(refs[u[0]][0], u) for u in units}
    k = {u: rows(refs[u[0]][1], u) for u in units}
    v = {u: rows(refs[u[0]][2], u) for u in units}
    decay = {u: jnp.exp(jnp.where(incl[u[0]], gcs[u] - gcs[u].T, -jnp.inf)) for u in units}
    eg = {u: jnp.exp(gcs[u]) for u in units}
    g_last = {u: gcs[u][last[u[0]]:last[u[0]] + 1, :] for u in units}
    kb = {u: k[u] * betas[u] for u in units}
    a = {u: jnp.where(strict[u[0]], -(_bdot(kb[u], k[u], NT_DIMS) * decay[u]), 0.0) for u in units}
    a_intra = {u: _bdot(q[u], k[u], NT_DIMS) * decay[u] for u in units}
    x = dict(a)
    for _ in range(int(math.log2(c)) - 1):
        a = {u: _bdot(a[u], a[u]) for u in units}
        x = {u: x[u] + a[u] + _bdot(x[u], a[u]) for u in units}
    uw = {}
    for u in units:
        rhs = jnp.concatenate([v[u] * betas[u], kb[u] * eg[u]], axis=1)
        uw[u] = rhs + _bdot(x[u], rhs)
    wq = {u: jnp.concatenate([uw[u][:, DN_DV:], q[u] * eg[u]], axis=0) for u in units}
    k_tail = {u: k[u] * jnp.exp(g_last[u] - gcs[u]) for u in units}
    s_decay = {u: jnp.exp(g_last[u]) for u in units}

    for j in range(nc):
        step = [(0, j), (1, nc - 1 - j)]
        s = {u: refs[u[0]][5][...] for u in step}
        r = {u: _bdot(wq[u], s[u]) for u in step}
        v_new = {u: uw[u][:, :DN_DV] - r[u][:c] for u in step}
        for u in step:
            refs[u[0]][4][u[1] * c:(u[1] + 1) * c, :] = r[u][c:] + _bdot(a_intra[u], v_new[u])
        for u in step:
            refs[u[0]][5][...] = s[u] * s_decay[u] + _bdot(k_tail[u], v_new[u], TN_DIMS)


def _dnscan(dq, dk, dv, gates3, params):
    b, t, _ = dv.shape
    tt = min(8 * DN_CHUNK, t)
    nt = t // tt
    rep = DN_V_HEADS // DN_QK_HEADS
    fwd_qk = pl.BlockSpec((None, tt, DN_DK), lambda bi, h, i: (bi, i, h // rep))
    bwd_qk = pl.BlockSpec((None, tt, DN_DK), lambda bi, h, i: (bi, nt - 1 - i, h // rep))
    fwd_v = pl.BlockSpec((None, tt, DN_DV), lambda bi, h, i: (bi, i, h))
    bwd_v = pl.BlockSpec((None, tt, DN_DV), lambda bi, h, i: (bi, nt - 1 - i, h))
    fwd_g = pl.BlockSpec((None, tt, TAIL_WIDTH), lambda bi, h, i: (bi, i, 0))
    bwd_g = pl.BlockSpec((None, tt, TAIL_WIDTH), lambda bi, h, i: (bi, nt - 1 - i, 0))
    return pl.pallas_call(
        functools.partial(_dnscan_kernel, nc=tt // DN_CHUNK),
        grid=(b, DN_V_HEADS, nt),
        in_specs=[fwd_qk, fwd_qk, fwd_v, fwd_g, bwd_qk, bwd_qk, bwd_v, bwd_g,
                  pl.BlockSpec((SUBLANES, LANES), lambda bi, h, i: (0, 0))],
        out_specs=[fwd_v, bwd_v],
        out_shape=[jax.ShapeDtypeStruct((b, t, DN_WIDTH), F32)] * 2,
        scratch_shapes=[pltpu.VMEM((DN_DK, DN_DV), F32)] * 2,
        compiler_params=_cparams(("parallel", "parallel", "arbitrary")),
        name="dnscan",
    )(dq, dk, dv, gates3, dq, dk, dv, gates3, params)


def _outproj_kernel(x_ref, g0_ref, b0_ref, oa_ref, of_ref, ob_ref, dz_ref, nw_ref, w_ref, g1_ref, b1_ref,
                    y_ref, yb_ref, ybt_ref):
    o = of_ref[...] + ob_ref[...]
    heads = []
    for h in range(DN_V_HEADS):
        sl = slice(h * DN_DV, (h + 1) * DN_DV)
        oh = o[:, sl]
        oh = oh * lax.rsqrt(jnp.mean(oh * oh, axis=-1, keepdims=True) + RMS_EPS) * nw_ref[...]
        z = dz_ref[:, sl]
        heads.append((oh * (z * _sigmoid(z))).astype(BF16))
    o_dn = jnp.concatenate(heads, axis=1)
    mix = jnp.dot(oa_ref[...], w_ref[0:ATTN_WIDTH, :], preferred_element_type=F32)
    mix = mix + jnp.dot(o_dn, w_ref[ATTN_WIDTH:ATTN_WIDTH + DN_WIDTH, :], preferred_element_type=F32)
    x0 = _layer_norm(x_ref[...], g0_ref[...], b0_ref[...])
    y = _layer_norm(DEEPNORM_ALPHA * x0 + mix, g1_ref[...], b1_ref[...])
    y_ref[...] = y
    yb_ref[...] = y.astype(BF16)
    ybt_ref[...] = y.T.astype(BF16)


def _outproj(x, g0, b0, o_attn, o_f, o_b, proj, norm_w, w_out, g1, b1):
    m, d = x.shape
    tm = min(256, m)
    row = lambda i: (i, 0)
    fixed = lambda i: (0, 0)
    return pl.pallas_call(
        _outproj_kernel,
        grid=(m // tm,),
        in_specs=[
            pl.BlockSpec((tm, d), row),
            pl.BlockSpec((1, d), fixed),
            pl.BlockSpec((1, d), fixed),
            pl.BlockSpec((tm, ATTN_WIDTH), row),
            pl.BlockSpec((tm, DN_WIDTH), row),
            pl.BlockSpec((tm, DN_WIDTH), row),
            pl.BlockSpec((tm, DN_WIDTH), lambda i: (i, ATTN_WIDTH // DN_WIDTH)),
            pl.BlockSpec((1, DN_DV), fixed),
            pl.BlockSpec((ATTN_WIDTH + DN_WIDTH, d), fixed),
            pl.BlockSpec((1, d), fixed),
            pl.BlockSpec((1, d), fixed),
        ],
        out_specs=[pl.BlockSpec((tm, d), row), pl.BlockSpec((tm, d), row), pl.BlockSpec((d, tm), lambda i: (0, i))],
        out_shape=[jax.ShapeDtypeStruct((m, d), F32), jax.ShapeDtypeStruct((m, d), BF16),
                   jax.ShapeDtypeStruct((d, m), BF16)],
        compiler_params=_cparams(("parallel",)),
        name="outproj",
    )(x, g0, b0, o_attn, o_f, o_b, proj, norm_w, w_out, g1, b1)


def _peerq_kernel(x_ref, wq_ref, keys_ref, s_ref, *, n_sets, n_keys, half):
    qry = jnp.dot(x_ref[...], wq_ref[...], preferred_element_type=F32).astype(BF16)
    for hp in range(n_sets):
        s_ref[hp * n_keys:(hp + 1) * n_keys, :] = lax.dot_general(
            keys_ref[hp], qry[:, hp * half:(hp + 1) * half], NT_DIMS, preferred_element_type=F32)


def _peerq(xb, wq, keys):
    m, d = xb.shape
    n_sets, n_keys, half = keys.shape
    tm = min(512, m)
    return pl.pallas_call(
        functools.partial(_peerq_kernel, n_sets=n_sets, n_keys=n_keys, half=half),
        grid=(m // tm,),
        in_specs=[
            pl.BlockSpec((tm, d), lambda i: (i, 0)),
            pl.BlockSpec((d, n_sets * half), lambda i: (0, 0)),
            pl.BlockSpec((n_sets, n_keys, half), lambda i: (0, 0, 0)),
        ],
        out_specs=pl.BlockSpec((n_sets * n_keys, tm), lambda i: (0, i)),
        out_shape=jax.ShapeDtypeStruct((n_sets * n_keys, m), F32),
        compiler_params=_cparams(("parallel",)),
        name="peerq",
    )(xb, wq, keys)


def _extract_top(work, n, ranked=False):
    tops = []
    rank = jnp.full(work.shape, PEER_NO_RANK, F32) if ranked else None
    for r in range(n):
        mx = jnp.max(work, axis=0, keepdims=True)
        tops.append(mx)
        hit = work == mx
        if ranked:
            rank = jnp.where(hit, float(r + 1), rank)
        work = jnp.where(hit, -jnp.inf, work)
    return (tops, rank) if ranked else tops


def _peertopk_kernel(s_ref, rows_ref, tiles_ref):
    k = PEER_TOPK
    for h in range(PEER_HEADS):
        s1 = s_ref[h, 0]
        s2 = s_ref[h, 1]
        a = _extract_top(s1, k)
        b, rank2 = _extract_top(s2, k, ranked=True)
        a_all = jnp.concatenate(a, axis=0)
        b_all = jnp.concatenate(b, axis=0)
        half = k // 2
        row = lax.broadcasted_iota(jnp.int32, (half, s1.shape[1]), 0)
        cands = [a_all + b[0]]
        for q in range(1, half):
            cands.append(jnp.where(row < k // (q + 1), a_all[:half] + b[q], -jnp.inf))
        cands.append(a[0] + b_all[half:])
        sel = _extract_top(jnp.concatenate(cands, axis=0), k)
        tau = sel[k - 1]
        z = jnp.exp(sel[0] - sel[0])
        for r in range(1, k):
            z = z + jnp.exp(sel[r] - sel[0])
        count = jnp.zeros(s1.shape, F32)
        for q in range(k):
            count = count + jnp.where(s1 + b[q] >= tau, 1.0, 0.0)
        rows_ref[0, h] = count
        rows_ref[1, h] = jnp.exp(s1 - a[0])
        tiles_ref[0, h] = rank2.astype(BF16)
        tiles_ref[1, h] = (jnp.exp(s2 - b[0]) * (1.0 / z)).astype(BF16)


def _peertopk(s4):
    nh, _, n_keys, m = s4.shape
    tl = min(256, m)
    blk = lambda i: (0, 0, 0, i)
    return pl.pallas_call(
        _peertopk_kernel,
        grid=(m // tl,),
        in_specs=[pl.BlockSpec((nh, 2, n_keys, tl), blk)],
        out_specs=[pl.BlockSpec((2, nh, n_keys, tl), blk), pl.BlockSpec((2, nh, n_keys, tl), blk)],
        out_shape=[jax.ShapeDtypeStruct((2, nh, n_keys, m), F32), jax.ShapeDtypeStruct((2, nh, n_keys, m), BF16)],
        compiler_params=_cparams(("parallel",)),
        name="peertopk",
    )(s4)


def _peerdense_kernel(xt_ref, u_ref, vta_ref, vtb_ref, rows_ref, tiles_ref, o_ref, acc_ref, a_ref, pre_ref, *,
                      n_keys, n_steps):
    e = pl.program_id(1)
    nh = PEER_HEADS
    sub = PEER_SUB
    n_slots = pre_ref.shape[0]
    n_i = sub // n_keys
    n_total = n_steps * n_slots
    tm = xt_ref.shape[1]

    @pl.when(e == 0)
    def _():
        acc_ref[...] = jnp.zeros(acc_ref.shape, F32)
        a_ref[...] = jnp.zeros(a_ref.shape, BF16)
        pre_ref[...] = jnp.zeros(pre_ref.shape, F32)

    def gated_act(g, slot):
        valid = jnp.logical_and(g >= 0, g < n_total)
        i0 = jnp.clip(g, 0, n_total - 1) * n_i
        tr = PEER_TILE_ROWS

        def row_tile(x):
            return jnp.broadcast_to(x, (tr, tm)).astype(BF16)

        cnt_rows = [[row_tile(jnp.where(valid, rows_ref[0, h, pl.ds(i0 + ii, 1), :], 0.0)) for h in range(nh)]
                    for ii in range(n_i)]
        e1_rows = [[row_tile(rows_ref[1, h, pl.ds(i0 + ii, 1), :]) for h in range(nh)] for ii in range(n_i)]
        for tg in range(tm // LANES):
            lanes = slice(tg * LANES, (tg + 1) * LANES)
            for r0 in range(0, n_keys, tr):
                gates = [jnp.zeros((tr, LANES), BF16) for _ in range(n_i)]
                for h in range(nh):
                    rank2 = tiles_ref[0, h, r0:r0 + tr, lanes]
                    e2 = tiles_ref[1, h, r0:r0 + tr, lanes]
                    for ii in range(n_i):
                        sel = rank2 <= cnt_rows[ii][h][:, lanes]
                        gates[ii] = gates[ii] + jnp.where(sel, e2 * e1_rows[ii][h][:, lanes], jnp.zeros_like(e2))
                for ii in range(n_i):
                    p = pre_ref[slot, ii * n_keys + r0:ii * n_keys + r0 + tr, lanes]
                    act = 0.5 * p * (1.0 + lax.erf(p * (2.0 ** -0.5)))
                    a_ref[slot * sub + ii * n_keys + r0:slot * sub + ii * n_keys + r0 + tr, lanes] = (
                        act.astype(BF16) * gates[ii])

    half = n_slots // 2
    for s in range(n_slots):
        g = e * n_slots + s
        pre_ref[s] = jnp.dot(u_ref[s * sub:(s + 1) * sub, :], xt_ref[...], preferred_element_type=F32)
        gated_act(g - 1, (s - 1) % n_slots)
        if s == half - 1:
            acc_ref[...] += jnp.dot(vta_ref[...], a_ref[half * sub:, :], preferred_element_type=F32)
        if s == n_slots - 1:
            acc_ref[...] += jnp.dot(vtb_ref[...], a_ref[:half * sub, :], preferred_element_type=F32)

    @pl.when(e == n_steps)
    def _():
        o_ref[...] = acc_ref[...].T


def _peerdense(xt, exp_u, exp_vt, sel_rows, sel_tiles):
    d, m = xt.shape
    n_exp = exp_u.shape[0]
    _, nh, n_keys, _ = sel_rows.shape
    tm = min(512, m)
    n_slots = 4
    eb = n_slots * PEER_SUB
    n_steps = n_exp // eb
    pair = eb // 2
    return pl.pallas_call(
        functools.partial(_peerdense_kernel, n_keys=n_keys, n_steps=n_steps),
        grid=(m // tm, n_steps + 1),
        in_specs=[
            pl.BlockSpec((d, tm), lambda i, e: (0, i)),
            pl.BlockSpec((eb, d), lambda i, e: (jnp.minimum(e, n_steps - 1), 0)),
            pl.BlockSpec((d, pair), lambda i, e: (0, jnp.maximum(2 * e - 1, 0))),
            pl.BlockSpec((d, pair), lambda i, e: (0, jnp.minimum(2 * e, 2 * n_steps - 1))),
            pl.BlockSpec((2, nh, n_keys, tm), lambda i, e: (0, 0, 0, i)),
            pl.BlockSpec((2, nh, n_keys, tm), lambda i, e: (0, 0, 0, i)),
        ],
        out_specs=pl.BlockSpec((tm, d), lambda i, e: (i, 0)),
        out_shape=jax.ShapeDtypeStruct((m, d), F32),
        scratch_shapes=[pltpu.VMEM((d, tm), F32), pltpu.VMEM((eb, tm), BF16),
                        pltpu.VMEM((n_slots, PEER_SUB, tm), F32)],
        compiler_params=_cparams(("parallel", "arbitrary")),
        name="peerdense",
    )(xt, exp_u, exp_vt, exp_vt, sel_rows, sel_tiles)


def _ln2_kernel(x_ref, p_ref, g_ref, b_ref, y_ref):
    y_ref[...] = _layer_norm(DEEPNORM_ALPHA * x_ref[...] + p_ref[...], g_ref[...], b_ref[...])


def _ln2(x, p, g, b):
    m, d = x.shape
    tm = min(512, m)
    row = lambda i: (i, 0)
    fixed = lambda i: (0, 0)
    return pl.pallas_call(
        _ln2_kernel,
        grid=(m // tm,),
        in_specs=[pl.BlockSpec((tm, d), row), pl.BlockSpec((tm, d), row),
                  pl.BlockSpec((1, d), fixed), pl.BlockSpec((1, d), fixed)],
        out_specs=pl.BlockSpec((tm, d), row),
        out_shape=jax.ShapeDtypeStruct((m, d), F32),
        compiler_params=_cparams(("parallel",)),
        name="ln2",
    )(x, p, g, b)


def _prepare_weights(ln_in_g, ln_in_b, w_in, dn_conv_w, dn_a_log, dn_dt_bias, dn_norm_w, w_out, ln1_g, ln1_b,
                     peer_wq, peer_keys, peer_u, peer_v, ln2_g, ln2_b):
    d = w_in.shape[0]
    o_ak = ATTN_WIDTH
    o_av = o_ak + ATTN_KV_WIDTH
    o_dqkv = o_av + ATTN_KV_WIDTH
    o_dz = o_dqkv + DN_CONV_DIM
    o_db = o_dz + DN_WIDTH
    w_main = jnp.concatenate(
        [w_in[:, :o_ak], w_in[:, o_dz:o_db], w_in[:, o_dqkv:o_dz], w_in[:, o_ak:o_av], w_in[:, o_av:o_dqkv]],
        axis=1).astype(BF16)
    n_gate = 4 * DN_V_HEADS
    w_tail = jnp.concatenate([w_in[:, o_db:o_db + n_gate], jnp.zeros((d, TAIL_WIDTH - n_gate), w_in.dtype)],
                             axis=1).astype(BF16)
    conv_w = jnp.concatenate([dn_conv_w, jnp.zeros((SUBLANES - CONV_W, DN_CONV_DIM), dn_conv_w.dtype)], axis=0)
    params = jnp.zeros((SUBLANES, LANES), F32)
    params = params.at[0:2, 0:DN_V_HEADS].set(dn_a_log).at[2:4, 0:DN_V_HEADS].set(dn_dt_bias)
    n_heads, _, n_keys, half = peer_keys.shape
    row = lambda v: v.reshape(1, -1)
    return dict(
        ln_in_g=row(ln_in_g), ln_in_b=row(ln_in_b), w_main=w_main, w_tail=w_tail, conv_w=conv_w, dn_params=params,
        dn_norm_w=row(dn_norm_w), w_out=w_out.astype(BF16), ln1_g=row(ln1_g), ln1_b=row(ln1_b),
        peer_wq=peer_wq.astype(BF16), peer_keys=peer_keys.reshape(n_heads * 2, n_keys, half).astype(BF16),
        peer_u=peer_u.astype(BF16), peer_vt=peer_v.astype(BF16).T, ln2_g=row(ln2_g), ln2_b=row(ln2_b))


def _trunk(x, w, q_norm_w, k_norm_w):
    b, t, d = x.shape
    m = b * t
    xf = x.reshape(m, d)
    proj, gates = _inproj(xf, w["ln_in_g"], w["ln_in_b"], w["w_main"], w["w_tail"])
    proj3 = proj.reshape(b, t, MAIN_WIDTH)
    o_attn = _attention(*_qkprep(proj3, q_norm_w, k_norm_w))
    dq, dk, dv = _dnprep(proj3, w["conv_w"])
    o_f, o_b = _dnscan(dq, dk, dv, gates.reshape(b, t, TAIL_WIDTH), w["dn_params"])
    x1, x1b, x1bt = _outproj(xf, w["ln_in_g"], w["ln_in_b"], o_attn, o_f.reshape(m, DN_WIDTH),
                             o_b.reshape(m, DN_WIDTH), proj, w["dn_norm_w"], w["w_out"], w["ln1_g"], w["ln1_b"])
    n_sets, n_keys, _ = w["peer_keys"].shape
    s4 = _peerq(x1b, w["peer_wq"], w["peer_keys"]).reshape(n_sets // 2, 2, n_keys, m)
    peer = _peerdense(x1bt, w["peer_u"], w["peer_vt"], *_peertopk(s4))
    y = _ln2(x1, peer, w["ln2_g"], w["ln2_b"])
    return y.reshape(b, t, d)


def kernel(x_prompt, x_sample, ln_in_g, ln_in_b, w_in, q_norm_w, k_norm_w, dn_conv_w, dn_a_log, dn_dt_bias,
           dn_norm_w, w_out, ln1_g, ln1_b, peer_wq, peer_keys, peer_u, peer_v, ln2_g, ln2_b):
    w = _prepare_weights(ln_in_g, ln_in_b, w_in[0], dn_conv_w[0], dn_a_log[0], dn_dt_bias[0], dn_norm_w[0],
                         w_out[0], ln1_g[0], ln1_b[0], peer_wq[0], peer_keys[0], peer_u[0], peer_v[0],
                         ln2_g[0], ln2_b[0])
    return (_trunk(x_prompt, w, q_norm_w[0], k_norm_w[0]), _trunk(x_sample, w, q_norm_w[0], k_norm_w[0]))
```

```python
import functools
import math

import jax
import jax.numpy as jnp
from jax import lax
from jax.experimental import pallas as pl
from jax.experimental.pallas import tpu as pltpu

F32 = jnp.float32
BF16 = jnp.bfloat16

HEAD_DIM = 128
ATTN_HEADS = 8
ATTN_KV_HEADS = 2
ATTN_GROUP = ATTN_HEADS // ATTN_KV_HEADS
ATTN_WIDTH = ATTN_HEADS * HEAD_DIM
ATTN_KV_WIDTH = ATTN_KV_HEADS * HEAD_DIM
DN_QK_HEADS = 4
DN_V_HEADS = 8
DN_DK = 128
DN_DV = 128
DN_QK_WIDTH = DN_QK_HEADS * DN_DK
DN_WIDTH = DN_V_HEADS * DN_DV
DN_CONV_DIM = 2 * DN_QK_WIDTH + DN_WIDTH
GRID_W = 64
ROPE_THETA = 10000.0
CONV_W = 5
PEER_HEADS = 8
PEER_TOPK = 16
DEPTH = 1
DEEPNORM_ALPHA = (2.0 * DEPTH) ** 0.25
LN_EPS = 1e-5
RMS_EPS = 1e-6
LOG2_E = math.log2(math.e)

LANES = 128
SUBLANES = 8
VMEM_LIMIT_BYTES = 56 * 1024 * 1024

MAIN_WIDTH = ATTN_WIDTH + DN_WIDTH + DN_CONV_DIM + 2 * ATTN_KV_WIDTH
TAIL_WIDTH = LANES
PEER_SUB = 256
PEER_NO_RANK = 64.0
PEER_TILE_ROWS = 16
V_ONES_ROWS = 16
DN_CHUNK = 128

NT_DIMS = (((1,), (1,)), ((), ()))
TN_DIMS = (((0,), (0,)), ((), ()))


def _cparams(sem):
    return pltpu.CompilerParams(dimension_semantics=sem, vmem_limit_bytes=VMEM_LIMIT_BYTES)


def _layer_norm(x, g, b):
    mu = jnp.mean(x, axis=-1, keepdims=True)
    xc = x - mu
    var = jnp.mean(xc * xc, axis=-1, keepdims=True)
    return xc * lax.rsqrt(var + LN_EPS) * g + b


def _sigmoid(x):
    return 1.0 / (1.0 + jnp.exp(-x))


def _bdot(a, b, dims=None):
    a = a.astype(BF16)
    b = b.astype(BF16)
    if dims is None:
        return jnp.dot(a, b, preferred_element_type=F32)
    return lax.dot_general(a, b, dims, preferred_element_type=F32)


def _inproj_kernel(x_ref, g_ref, b_ref, w_ref, wt_ref, o_ref, ot_ref, xn_ref):
    @pl.when(pl.program_id(1) == 0)
    def _():
        xn = _layer_norm(x_ref[...], g_ref[...], b_ref[...]).astype(BF16)
        xn_ref[...] = xn
        ot_ref[...] = jnp.dot(xn, wt_ref[...], preferred_element_type=F32)

    o_ref[...] = jnp.dot(xn_ref[...], w_ref[...], preferred_element_type=F32)


def _inproj(x, g, b, w_main, w_tail):
    m, d = x.shape
    tm = min(1024, m)
    tn = 512
    return pl.pallas_call(
        _inproj_kernel,
        grid=(m // tm, MAIN_WIDTH // tn),
        in_specs=[
            pl.BlockSpec((tm, d), lambda i, j: (i, 0)),
            pl.BlockSpec((1, d), lambda i, j: (0, 0)),
            pl.BlockSpec((1, d), lambda i, j: (0, 0)),
            pl.BlockSpec((d, tn), lambda i, j: (0, j)),
            pl.BlockSpec((d, TAIL_WIDTH), lambda i, j: (0, 0)),
        ],
        out_specs=[
            pl.BlockSpec((tm, tn), lambda i, j: (i, j)),
            pl.BlockSpec((tm, TAIL_WIDTH), lambda i, j: (i, 0)),
        ],
        out_shape=[
            jax.ShapeDtypeStruct((m, MAIN_WIDTH), F32),
            jax.ShapeDtypeStruct((m, TAIL_WIDTH), F32),
        ],
        scratch_shapes=[pltpu.VMEM((tm, d), BF16)],
        compiler_params=_cparams(("parallel", "arbitrary")),
        name="inproj",
    )(x, g, b, w_main, w_tail)


def _qkprep_kernel(aq_ref, ak_ref, av_ref, cos_ref, sin_ref, qw_ref, kw_ref, q_ref, k_ref, v_ref):
    cos = cos_ref[...]
    sin = sin_ref[...]
    lane = lax.broadcasted_iota(jnp.int32, cos.shape, 1)
    first = (lane % (HEAD_DIM // 2)) < (HEAD_DIM // 4)

    def prep(x, w, scale):
        ms = jnp.mean(x * x, axis=-1, keepdims=True)
        y = x * lax.rsqrt(ms + RMS_EPS) * w
        sw = jnp.where(first, pltpu.roll(y, LANES - HEAD_DIM // 4, 1), pltpu.roll(y, HEAD_DIM // 4, 1))
        r = y * cos + sw * sin
        return r * scale

    for h in range(ATTN_HEADS):
        sl = slice(h * HEAD_DIM, (h + 1) * HEAD_DIM)
        q_ref[h] = prep(aq_ref[:, sl], qw_ref[...], HEAD_DIM ** -0.5 * LOG2_E).T.astype(BF16)
    for h in range(ATTN_KV_HEADS):
        sl = slice(h * HEAD_DIM, (h + 1) * HEAD_DIM)
        k_ref[:, sl] = prep(ak_ref[:, sl], kw_ref[...], 1.0).astype(BF16)
        v_ref[h, 0:HEAD_DIM, :] = av_ref[:, sl].T.astype(BF16)
        v_ref[h, HEAD_DIM:, :] = jnp.ones((V_ONES_ROWS, v_ref.shape[2]), BF16)


def _rope_tables(t):
    half = HEAD_DIM // 4
    inv = ROPE_THETA ** (-jnp.arange(half, dtype=F32) * (2.0 / (HEAD_DIM // 2)))
    pos = jnp.arange(t, dtype=jnp.int32)
    rows = (pos // GRID_W).astype(F32)[:, None] * inv[None, :]
    cols = (pos % GRID_W).astype(F32)[:, None] * inv[None, :]
    cos = jnp.concatenate([jnp.cos(rows), jnp.cos(rows), jnp.cos(cols), jnp.cos(cols)], axis=-1)
    sin = jnp.concatenate([-jnp.sin(rows), jnp.sin(rows), -jnp.sin(cols), jnp.sin(cols)], axis=-1)
    return cos, sin


def _qkprep(proj3, q_norm_w, k_norm_w):
    b, t, _ = proj3.shape
    tm = min(512, t)
    cos, sin = _rope_tables(t)
    kv_blk = (MAIN_WIDTH - 2 * ATTN_KV_WIDTH) // ATTN_KV_WIDTH
    return pl.pallas_call(
        _qkprep_kernel,
        grid=(b, t // tm),
        in_specs=[
            pl.BlockSpec((None, tm, ATTN_WIDTH), lambda bi, i: (bi, i, 0)),
            pl.BlockSpec((None, tm, ATTN_KV_WIDTH), lambda bi, i: (bi, i, kv_blk)),
            pl.BlockSpec((None, tm, ATTN_KV_WIDTH), lambda bi, i: (bi, i, kv_blk + 1)),
            pl.BlockSpec((tm, HEAD_DIM), lambda bi, i: (i, 0)),
            pl.BlockSpec((tm, HEAD_DIM), lambda bi, i: (i, 0)),
            pl.BlockSpec((1, HEAD_DIM), lambda bi, i: (0, 0)),
            pl.BlockSpec((1, HEAD_DIM), lambda bi, i: (0, 0)),
        ],
        out_specs=[
            pl.BlockSpec((None, ATTN_HEADS, HEAD_DIM, tm), lambda bi, i: (bi, 0, 0, i)),
            pl.BlockSpec((None, tm, ATTN_KV_WIDTH), lambda bi, i: (bi, i, 0)),
            pl.BlockSpec((None, ATTN_KV_HEADS, HEAD_DIM + V_ONES_ROWS, tm), lambda bi, i: (bi, 0, 0, i)),
        ],
        out_shape=[
            jax.ShapeDtypeStruct((b, ATTN_HEADS, HEAD_DIM, t), BF16),
            jax.ShapeDtypeStruct((b, t, ATTN_KV_WIDTH), BF16),
            jax.ShapeDtypeStruct((b, ATTN_KV_HEADS, HEAD_DIM + V_ONES_ROWS, t), BF16),
        ],
        compiler_params=_cparams(("parallel", "parallel")),
        name="qkprep",
    )(proj3, proj3, proj3, cos, sin, q_norm_w.reshape(1, HEAD_DIM), k_norm_w.reshape(1, HEAD_DIM))


def _attn_kernel(qt_ref, k_ref, vt_ref, o_ref, m_ref, acc_ref, *, tk, cpb, nk):
    m_ref[...] = jnp.full(m_ref.shape, -jnp.inf, F32)
    acc_ref[...] = jnp.zeros(acc_ref.shape, F32)

    def body(c, carry):
        ks, vts = [], []
        for j in range(cpb):
            off = pl.multiple_of((c * cpb + j) * tk, tk)
            ks.append(k_ref[pl.ds(off, tk), :])
            vts.append(vt_ref[:, pl.ds(off, tk)])

        def scores(j, g):
            return jnp.dot(ks[j], qt_ref[g], preferred_element_type=F32)

        def softmax(g, s):
            m_prev = m_ref[g]
            m_new = jnp.maximum(m_prev, jnp.max(s, axis=0, keepdims=True))
            m_ref[g] = m_new
            return jnp.exp2(m_prev - m_new), jnp.exp2((s - m_new).astype(BF16))

        def values(j, g, alpha, p):
            acc_ref[g] = acc_ref[g] * alpha + jnp.dot(vts[j], p, preferred_element_type=F32)

        tiles = [(j, g) for j in range(cpb) for g in range(ATTN_GROUP)]
        s = {i: scores(*tiles[i]) for i in range(2)}
        ap = {}
        for i, (j, g) in enumerate(tiles):
            ap[i] = softmax(g, s.pop(i))
            if i + 2 < len(tiles):
                s[i + 2] = scores(*tiles[i + 2])
            if i >= 1:
                values(*tiles[i - 1], *ap.pop(i - 1))
        values(*tiles[-1], *ap.pop(len(tiles) - 1))
        return carry

    lax.fori_loop(0, nk // cpb, body, 0)
    for g in range(ATTN_GROUP):
        out = acc_ref[g, 0:HEAD_DIM, :] * (1.0 / acc_ref[g, HEAD_DIM:HEAD_DIM + 1, :])
        o_ref[:, g * HEAD_DIM:(g + 1) * HEAD_DIM] = out.T.astype(o_ref.dtype)


def _attention(qt, k, vt):
    b, _, _, t = qt.shape
    vrows = vt.shape[2]
    tq = min(256, t)
    tk = min(512, t)
    nk = t // tk
    cpb = math.gcd(nk, 8)
    gw = ATTN_GROUP * HEAD_DIM
    out = pl.pallas_call(
        functools.partial(_attn_kernel, tk=tk, cpb=cpb, nk=nk),
        grid=(b, ATTN_KV_HEADS, t // tq),
        in_specs=[
            pl.BlockSpec((None, ATTN_GROUP, HEAD_DIM, tq), lambda bi, kv, qi: (bi, kv, 0, qi)),
            pl.BlockSpec((None, t, HEAD_DIM), lambda bi, kv, qi: (bi, 0, kv)),
            pl.BlockSpec((None, None, vrows, t), lambda bi, kv, qi: (bi, kv, 0, 0)),
        ],
        out_specs=pl.BlockSpec((None, tq, gw), lambda bi, kv, qi: (bi, qi, kv)),
        out_shape=jax.ShapeDtypeStruct((b, t, ATTN_WIDTH), BF16),
        scratch_shapes=[
            pltpu.VMEM((ATTN_GROUP, 1, tq), F32),
            pltpu.VMEM((ATTN_GROUP, vrows, tq), F32),
        ],
        compiler_params=_cparams(("parallel", "parallel", "arbitrary")),
        name="attn",
    )(qt, k, vt)
    return out.reshape(b * t, ATTN_WIDTH)


def _dnprep_kernel(x_ref, prev_ref, next_ref, cw_ref, q_ref, k_ref, v_ref, xe_ref, *, tt, nt):
    i = pl.program_id(1)
    halo = SUBLANES
    pad = CONV_W // 2
    xe_ref[0:halo, :] = jnp.where(i > 0, prev_ref[...], 0.0)
    xe_ref[halo:halo + tt, :] = x_ref[...]
    xe_ref[halo + tt:halo + tt + halo, :] = jnp.where(i < nt - 1, next_ref[...], 0.0)
    n_qk = DN_QK_WIDTH // DN_DK
    for c in range(DN_CONV_DIM // LANES):
        sl = slice(c * LANES, (c + 1) * LANES)
        acc = xe_ref[halo - pad:halo - pad + tt, sl] * cw_ref[0:1, sl]
        for w in range(1, CONV_W):
            acc = acc + xe_ref[halo - pad + w:halo - pad + w + tt, sl] * cw_ref[w:w + 1, sl]
        h = acc * _sigmoid(acc)
        if c < 2 * n_qk:
            h = h * lax.rsqrt(jnp.sum(h * h, axis=-1, keepdims=True) + RMS_EPS)
        if c < n_qk:
            q_ref[:, sl] = h * (DN_DK ** -0.5)
        elif c < 2 * n_qk:
            k_ref[:, (c - n_qk) * LANES:(c - n_qk + 1) * LANES] = h
        else:
            v_ref[:, (c - 2 * n_qk) * LANES:(c - 2 * n_qk + 1) * LANES] = h


def _dnprep(proj3, conv_w):
    b, t, _ = proj3.shape
    tt = min(256, t)
    nt = t // tt
    hb = tt // SUBLANES
    blk = (ATTN_WIDTH + DN_WIDTH) // DN_CONV_DIM
    return pl.pallas_call(
        functools.partial(_dnprep_kernel, tt=tt, nt=nt),
        grid=(b, nt),
        in_specs=[
            pl.BlockSpec((None, tt, DN_CONV_DIM), lambda bi, i: (bi, i, blk)),
            pl.BlockSpec((None, SUBLANES, DN_CONV_DIM), lambda bi, i: (bi, jnp.maximum(i * hb - 1, 0), blk)),
            pl.BlockSpec((None, SUBLANES, DN_CONV_DIM),
                         lambda bi, i: (bi, jnp.minimum((i + 1) * hb, t // SUBLANES - 1), blk)),
            pl.BlockSpec((SUBLANES, DN_CONV_DIM), lambda bi, i: (0, 0)),
        ],
        out_specs=[
            pl.BlockSpec((None, tt, DN_QK_WIDTH), lambda bi, i: (bi, i, 0)),
            pl.BlockSpec((None, tt, DN_QK_WIDTH), lambda bi, i: (bi, i, 0)),
            pl.BlockSpec((None, tt, DN_WIDTH), lambda bi, i: (bi, i, 0)),
        ],
        out_shape=[
            jax.ShapeDtypeStruct((b, t, DN_QK_WIDTH), F32),
            jax.ShapeDtypeStruct((b, t, DN_QK_WIDTH), F32),
            jax.ShapeDtypeStruct((b, t, DN_WIDTH), F32),
        ],
        scratch_shapes=[pltpu.VMEM((tt + 2 * SUBLANES, DN_CONV_DIM), F32)],
        compiler_params=_cparams(("parallel", "arbitrary")),
        name="dnprep",
    )(proj3, proj3, proj3, conv_w)


def _pick_lane(x, idx):
    lane = lax.broadcasted_iota(jnp.int32, x.shape, x.ndim - 1)
    return jnp.sum(jnp.where(lane == idx, x, 0.0), axis=-1, keepdims=True)


def _exact_tri_dot(tri, x):
    hi = x.astype(BF16)
    r1 = x - hi.astype(F32)
    mid = r1.astype(BF16)
    lo = (r1 - mid.astype(F32)).astype(BF16)
    t = tri.astype(BF16)
    out = jnp.dot(t, hi, preferred_element_type=F32)
    out = out + jnp.dot(t, mid, preferred_element_type=F32)
    return out + jnp.dot(t, lo, preferred_element_type=F32)


def _dnscan_kernel(qf_ref, kf_ref, vf_ref, gf_ref, qb_ref, kb_ref, vb_ref, gb_ref, par_ref,
                   of_ref, ob_ref, sf_ref, sb_ref, *, nc):
    h = pl.program_id(1)
    c = DN_CHUNK

    @pl.when(pl.program_id(2) == 0)
    def _():
        sf_ref[...] = jnp.zeros(sf_ref.shape, F32)
        sb_ref[...] = jnp.zeros(sb_ref.shape, F32)

    row = lax.broadcasted_iota(jnp.int32, (c, c), 0)
    col = lax.broadcasted_iota(jnp.int32, (c, c), 1)
    incl = (col <= row, col >= row)
    strict = (col < row, col > row)
    last = (c - 1, 0)
    refs = ((qf_ref, kf_ref, vf_ref, gf_ref, of_ref, sf_ref), (qb_ref, kb_ref, vb_ref, gb_ref, ob_ref, sb_ref))
    par = par_ref[...]

    units = []
    gcs = {}
    betas = {}
    for d, (q_ref, k_ref, v_ref, g_ref, o_ref, s_ref) in enumerate(refs):
        a_log = _pick_lane(par[d:d + 1, :], h)
        dt_bias = _pick_lane(par[2 + d:3 + d, :], h)
        gates = g_ref[...]
        beta = _sigmoid(_pick_lane(gates, d * DN_V_HEADS + h))
        da = _pick_lane(gates, (2 + d) * DN_V_HEADS + h) + dt_bias
        softplus = jnp.maximum(da, 0.0) + jnp.log(1.0 + jnp.exp(-jnp.abs(da)))
        g = -jnp.exp(a_log) * softplus
        g_cols = jnp.zeros((c, c), F32)
        for ci in range(nc):
            g_cols = jnp.where(col == ci, g[ci * c:(ci + 1) * c, :], g_cols)
        gc_all = _exact_tri_dot(incl[d], g_cols)
        for j in range(nc):
            ci = nc - 1 - j if d else j
            units.append((d, ci))
            gcs[d, ci] = jnp.broadcast_to(gc_all[:, ci:ci + 1], (c, c))
            betas[d, ci] = beta[ci * c:(ci + 1) * c, :]

    def rows(ref, u):
        return ref[u[1] * c:(u[1] + 1) * c, :]

    q = {u: rows(refs[u[0]][0], u) for u in units}
    k = {u: rows(refs[u[0]][1], u) for u in units}
    v = {u: rows(refs[u[0]][2], u) for u in units}
    decay = {u: jnp.exp(jnp.where(incl[u[0]], gcs[u] - gcs[u].T, -jnp.inf)) for u in units}
    eg = {u: jnp.exp(gcs[u]) for u in units}
    g_last = {u: gcs[u][last[u[0]]:last[u[0]] + 1, :] for u in units}
    kb = {u: k[u] * betas[u] for u in units}
    a = {u: jnp.where(strict[u[0]], -(_bdot(kb[u], k[u], NT_DIMS) * decay[u]), 0.0) for u in units}
    a_intra = {u: _bdot(q[u], k[u], NT_DIMS) * decay[u] for u in units}
    x = dict(a)
    for _ in range(int(math.log2(c)) - 1):
        a = {u: _bdot(a[u], a[u]) for u in units}
        x = {u: x[u] + a[u] + _bdot(x[u], a[u]) for u in units}
    uw = {}
    for u in units:
        rhs = jnp.concatenate([v[u] * betas[u], kb[u] * eg[u]], axis=1)
        uw[u] = rhs + _bdot(x[u], rhs)
    wq = {u: jnp.concatenate([uw[u][:, DN_DV:], q[u] * eg[u]], axis=0) for u in units}
    k_tail = {u: k[u] * jnp.exp(g_last[u] - gcs[u]) for u in units}
    s_decay = {u: jnp.exp(g_last[u]) for u in units}

    for j in range(nc):
        step = [(0, j), (1, nc - 1 - j)]
        s = {u: refs[u[0]][5][...] for u in step}
        r = {u: _bdot(wq[u], s[u]) for u in step}
        v_new = {u: uw[u][:, :DN_DV] - r[u][:c] for u in step}
        for u in step:
            refs[u[0]][4][u[1] * c:(u[1] + 1) * c, :] = r[u][c:] + _bdot(a_intra[u], v_new[u])
        for u in step:
            refs[u[0]][5][...] = s[u] * s_decay[u] + _bdot(k_tail[u], v_new[u], TN_DIMS)


def _dnscan(dq, dk, dv, gates3, params):
    b, t, _ = dv.shape
    tt = min(8 * DN_CHUNK, t)
    nt = t // tt
    rep = DN_V_HEADS // DN_QK_HEADS
    fwd_qk = pl.BlockSpec((None, tt, DN_DK), lambda bi, h, i: (bi, i, h // rep))
    bwd_qk = pl.BlockSpec((None, tt, DN_DK), lambda bi, h, i: (bi, nt - 1 - i, h // rep))
    fwd_v = pl.BlockSpec((None, tt, DN_DV), lambda bi, h, i: (bi, i, h))
    bwd_v = pl.BlockSpec((None, tt, DN_DV), lambda bi, h, i: (bi, nt - 1 - i, h))
    fwd_g = pl.BlockSpec((None, tt, TAIL_WIDTH), lambda bi, h, i: (bi, i, 0))
    bwd_g = pl.BlockSpec((None, tt, TAIL_WIDTH), lambda bi, h, i: (bi, nt - 1 - i, 0))
    return pl.pallas_call(
        functools.partial(_dnscan_kernel, nc=tt // DN_CHUNK),
        grid=(b, DN_V_HEADS, nt),
        in_specs=[fwd_qk, fwd_qk, fwd_v, fwd_g, bwd_qk, bwd_qk, bwd_v, bwd_g,
                  pl.BlockSpec((SUBLANES, LANES), lambda bi, h, i: (0, 0))],
        out_specs=[fwd_v, bwd_v],
        out_shape=[jax.ShapeDtypeStruct((b, t, DN_WIDTH), F32)] * 2,
        scratch_shapes=[pltpu.VMEM((DN_DK, DN_DV), F32)] * 2,
        compiler_params=_cparams(("parallel", "parallel", "arbitrary")),
        name="dnscan",
    )(dq, dk, dv, gates3, dq, dk, dv, gates3, params)


def _outproj_kernel(x_ref, g0_ref, b0_ref, oa_ref, of_ref, ob_ref, dz_ref, nw_ref, w_ref, g1_ref, b1_ref,
                    y_ref, yb_ref, ybt_ref):
    o = of_ref[...] + ob_ref[...]
    heads = []
    for h in range(DN_V_HEADS):
        sl = slice(h * DN_DV, (h + 1) * DN_DV)
        oh = o[:, sl]
        oh = oh * lax.rsqrt(jnp.mean(oh * oh, axis=-1, keepdims=True) + RMS_EPS) * nw_ref[...]
        z = dz_ref[:, sl]
        heads.append((oh * (z * _sigmoid(z))).astype(BF16))
    o_dn = jnp.concatenate(heads, axis=1)
    mix = jnp.dot(oa_ref[...], w_ref[0:ATTN_WIDTH, :], preferred_element_type=F32)
    mix = mix + jnp.dot(o_dn, w_ref[ATTN_WIDTH:ATTN_WIDTH + DN_WIDTH, :], preferred_element_type=F32)
    x0 = _layer_norm(x_ref[...], g0_ref[...], b0_ref[...])
    y = _layer_norm(DEEPNORM_ALPHA * x0 + mix, g1_ref[...], b1_ref[...])
    y_ref[...] = y
    yb_ref[...] = y.astype(BF16)
    ybt_ref[...] = y.T.astype(BF16)


def _outproj(x, g0, b0, o_attn, o_f, o_b, proj, norm_w, w_out, g1, b1):
    m, d = x.shape
    tm = min(256, m)
    row = lambda i: (i, 0)
    fixed = lambda i: (0, 0)
    return pl.pallas_call(
        _outproj_kernel,
        grid=(m // tm,),
        in_specs=[
            pl.BlockSpec((tm, d), row),
            pl.BlockSpec((1, d), fixed),
            pl.BlockSpec((1, d), fixed),
            pl.BlockSpec((tm, ATTN_WIDTH), row),
            pl.BlockSpec((tm, DN_WIDTH), row),
            pl.BlockSpec((tm, DN_WIDTH), row),
            pl.BlockSpec((tm, DN_WIDTH), lambda i: (i, ATTN_WIDTH // DN_WIDTH)),
            pl.BlockSpec((1, DN_DV), fixed),
            pl.BlockSpec((ATTN_WIDTH + DN_WIDTH, d), fixed),
            pl.BlockSpec((1, d), fixed),
            pl.BlockSpec((1, d), fixed),
        ],
        out_specs=[pl.BlockSpec((tm, d), row), pl.BlockSpec((tm, d), row), pl.BlockSpec((d, tm), lambda i: (0, i))],
        out_shape=[jax.ShapeDtypeStruct((m, d), F32), jax.ShapeDtypeStruct((m, d), BF16),
                   jax.ShapeDtypeStruct((d, m), BF16)],
        compiler_params=_cparams(("parallel",)),
        name="outproj",
    )(x, g0, b0, o_attn, o_f, o_b, proj, norm_w, w_out, g1, b1)


def _peerq_kernel(x_ref, wq_ref, keys_ref, s_ref, *, n_sets, n_keys, half):
    qry = jnp.dot(x_ref[...], wq_ref[...], preferred_element_type=F32).astype(BF16)
    for hp in range(n_sets):
        s_ref[hp * n_keys:(hp + 1) * n_keys, :] = lax.dot_general(
            keys_ref[hp], qry[:, hp * half:(hp + 1) * half], NT_DIMS, preferred_element_type=F32)


def _peerq(xb, wq, keys):
    m, d = xb.shape
    n_sets, n_keys, half = keys.shape
    tm = min(512, m)
    return pl.pallas_call(
        functools.partial(_peerq_kernel, n_sets=n_sets, n_keys=n_keys, half=half),
        grid=(m // tm,),
        in_specs=[
            pl.BlockSpec((tm, d), lambda i: (i, 0)),
            pl.BlockSpec((d, n_sets * half), lambda i: (0, 0)),
            pl.BlockSpec((n_sets, n_keys, half), lambda i: (0, 0, 0)),
        ],
        out_specs=pl.BlockSpec((n_sets * n_keys, tm), lambda i: (0, i)),
        out_shape=jax.ShapeDtypeStruct((n_sets * n_keys, m), F32),
        compiler_params=_cparams(("parallel",)),
        name="peerq",
    )(xb, wq, keys)


def _extract_top(work, n, ranked=False):
    tops = []
    rank = jnp.full(work.shape, PEER_NO_RANK, F32) if ranked else None
    for r in range(n):
        mx = jnp.max(work, axis=0, keepdims=True)
        tops.append(mx)
        hit = work == mx
        if ranked:
            rank = jnp.where(hit, float(r + 1), rank)
        work = jnp.where(hit, -jnp.inf, work)
    return (tops, rank) if ranked else tops


def _extract_top_stable(work, order, n):
    tops = []
    rank = jnp.full(work.shape, PEER_NO_RANK, F32)
    for r in range(n):
        mx = jnp.max(work, axis=0, keepdims=True)
        tops.append(mx)
        first = jnp.min(jnp.where(work == mx, order, jnp.inf), axis=0, keepdims=True)
        hit = order == first
        rank = jnp.where(hit, float(r + 1), rank)
        work = jnp.where(hit, -jnp.inf, work)
    return tops, rank


def _pair_candidates(a, b):
    k = PEER_TOPK
    half = k // 2
    tl = a[0].shape[1]
    a_all = jnp.concatenate(a, axis=0)
    b_all = jnp.concatenate(b, axis=0)
    row16 = lax.broadcasted_iota(jnp.int32, (k, tl), 0)
    row8 = lax.broadcasted_iota(jnp.int32, (half, tl), 0)
    cands = [a_all + b[0]]
    flat = [row16 * k]
    for q in range(1, half):
        cands.append(jnp.where(row8 < k // (q + 1), a_all[:half] + b[q], -jnp.inf))
        flat.append(row8 * k + q)
    cands.append(a[0] + b_all[half:])
    flat.append(row8 + half)
    return jnp.concatenate(cands, axis=0), jnp.concatenate(flat, axis=0).astype(F32)


def _count_ge(x, v):
    return jnp.sum(jnp.where(x >= v, 1.0, 0.0), axis=0, keepdims=True)


def _softmax_norm(sel):
    z = jnp.exp(sel[0] - sel[0])
    for r in range(1, len(sel)):
        z = z + jnp.exp(sel[r] - sel[0])
    return 1.0 / z


def _peertopk_kernel(s_ref, rows_ref, tiles_ref):
    k = PEER_TOPK
    half = k // 2
    for h in range(PEER_HEADS):
        s1 = s_ref[h, 0]
        s2 = s_ref[h, 1]
        a = _extract_top(s1, k)
        b, rank2 = _extract_top(s2, k, ranked=True)
        cand, flat = _pair_candidates(a, b)
        sel = _extract_top(cand, k)
        tau = sel[k - 1]
        count = jnp.zeros(s1.shape, F32)
        for q in range(k):
            count = count + jnp.where(s1 + b[q] >= tau, 1.0, 0.0)
        rows_ref[0, h] = count
        rows_ref[1, h] = jnp.exp(s1 - a[0])
        tiles_ref[0, h] = rank2.astype(BF16)
        tiles_ref[1, h] = (jnp.exp(s2 - b[0]) * _softmax_norm(sel)).astype(BF16)

        tied = jnp.maximum(jnp.maximum(jnp.abs(_count_ge(s1, a[k - 1]) - k), jnp.abs(_count_ge(s2, b[k - 1]) - k)),
                           jnp.abs(_count_ge(cand, tau) - k))
        any_tied = tied if h == 0 else jnp.maximum(any_tied, tied)

    @pl.when(jnp.max(any_tied) > 0.0)
    def _():
        for h in range(PEER_HEADS):
            s1 = s_ref[h, 0]
            s2 = s_ref[h, 1]
            key = lax.broadcasted_iota(jnp.int32, s1.shape, 0).astype(F32)
            a, rank1 = _extract_top_stable(s1, key, k)
            b, rank2 = _extract_top_stable(s2, key, k)
            cand, flat = _pair_candidates(a, b)
            sel, crank = _extract_top_stable(cand, flat, k)
            chosen = jnp.where(crank <= float(k), 1.0, 0.0)
            per_rank = chosen[:k]
            low = chosen[k:k + half]
            for q in range(2, half):
                low = low + chosen[k + (q - 1) * half:k + q * half]
            tail = jnp.sum(chosen[k + (half - 1) * half:], axis=0, keepdims=True)
            row = lax.broadcasted_iota(jnp.int32, per_rank.shape, 0)
            per_rank = per_rank + jnp.concatenate([low, jnp.zeros_like(low)], axis=0) + jnp.where(row == 0, tail, 0.0)
            count = jnp.zeros(s1.shape, F32)
            for r in range(k):
                count = jnp.where(rank1 == float(r + 1), per_rank[r:r + 1], count)
            rows_ref[0, h] = count
            rows_ref[1, h] = jnp.exp(s1 - a[0])
            tiles_ref[0, h] = rank2.astype(BF16)
            tiles_ref[1, h] = (jnp.exp(s2 - b[0]) * _softmax_norm(sel)).astype(BF16)


def _peertopk(s4):
    nh, _, n_keys, m = s4.shape
    tl = min(256, m)
    blk = lambda i: (0, 0, 0, i)
    return pl.pallas_call(
        _peertopk_kernel,
        grid=(m // tl,),
        in_specs=[pl.BlockSpec((nh, 2, n_keys, tl), blk)],
        out_specs=[pl.BlockSpec((2, nh, n_keys, tl), blk), pl.BlockSpec((2, nh, n_keys, tl), blk)],
        out_shape=[jax.ShapeDtypeStruct((2, nh, n_keys, m), F32), jax.ShapeDtypeStruct((2, nh, n_keys, m), BF16)],
        compiler_params=_cparams(("parallel",)),
        name="peertopk",
    )(s4)


def _peerdense_kernel(xt_ref, u_ref, vta_ref, vtb_ref, rows_ref, tiles_ref, o_ref, acc_ref, *slot_refs,
                      n_keys, n_steps):
    e = pl.program_id(1)
    nh = PEER_HEADS
    sub = PEER_SUB
    n_slots = len(slot_refs) // 2
    pre_refs = slot_refs[:n_slots]
    a_refs = slot_refs[n_slots:]
    n_i = sub // n_keys
    n_total = n_steps * n_slots
    tm = xt_ref.shape[1]

    @pl.when(e == 0)
    def _():
        acc_ref[...] = jnp.zeros(acc_ref.shape, F32)
        for r in pre_refs:
            r[...] = jnp.zeros(r.shape, F32)
        for r in a_refs:
            r[...] = jnp.zeros(r.shape, BF16)

    def gated_act(g, pre_ref, a_ref):
        valid = jnp.logical_and(g >= 0, g < n_total)
        i0 = jnp.clip(g, 0, n_total - 1) * n_i
        tr = PEER_TILE_ROWS

        def row_tile(x):
            return jnp.broadcast_to(x, (tr, tm)).astype(BF16)

        cnt_rows = [[row_tile(jnp.where(valid, rows_ref[0, h, pl.ds(i0 + ii, 1), :], 0.0)) for h in range(nh)]
                    for ii in range(n_i)]
        e1_rows = [[row_tile(rows_ref[1, h, pl.ds(i0 + ii, 1), :]) for h in range(nh)] for ii in range(n_i)]
        for tg in range(tm // LANES):
            lanes = slice(tg * LANES, (tg + 1) * LANES)
            for r0 in range(0, n_keys, tr):
                gates = [jnp.zeros((tr, LANES), BF16) for _ in range(n_i)]
                for h in range(nh):
                    rank2 = tiles_ref[0, h, r0:r0 + tr, lanes]
                    e2 = tiles_ref[1, h, r0:r0 + tr, lanes]
                    for ii in range(n_i):
                        sel = rank2 <= cnt_rows[ii][h][:, lanes]
                        gates[ii] = gates[ii] + jnp.where(sel, e2 * e1_rows[ii][h][:, lanes], jnp.zeros_like(e2))
                for ii in range(n_i):
                    p = pre_ref[ii * n_keys + r0:ii * n_keys + r0 + tr, lanes]
                    act = 0.5 * p * (1.0 + lax.erf(p * (2.0 ** -0.5)))
                    a_ref[ii * n_keys + r0:ii * n_keys + r0 + tr, lanes] = act.astype(BF16) * gates[ii]

    half = n_slots // 2
    for s in range(n_slots):
        g = e * n_slots + s
        pre_refs[s][...] = jnp.dot(u_ref[s * sub:(s + 1) * sub, :], xt_ref[...], preferred_element_type=F32)
        gated_act(g - 1, pre_refs[(s - 1) % n_slots], a_refs[(s - 1) % n_slots])
        vt_ref = vta_ref if s < half else vtb_ref
        vcols = slice((s % half) * sub, (s % half + 1) * sub)
        acc_ref[...] += jnp.dot(vt_ref[:, vcols], a_refs[(s - 2) % n_slots][...], preferred_element_type=F32)

    @pl.when(e == n_steps)
    def _():
        o_ref[...] = acc_ref[...].T


def _peerdense(xt, exp_u, exp_vt, sel_rows, sel_tiles):
    d, m = xt.shape
    n_exp = exp_u.shape[0]
    _, nh, n_keys, _ = sel_rows.shape
    tm = min(512, m)
    n_slots = 4
    eb = n_slots * PEER_SUB
    n_steps = n_exp // eb
    pair = eb // 2
    return pl.pallas_call(
        functools.partial(_peerdense_kernel, n_keys=n_keys, n_steps=n_steps),
        grid=(m // tm, n_steps + 1),
        in_specs=[
            pl.BlockSpec((d, tm), lambda i, e: (0, i)),
            pl.BlockSpec((eb, d), lambda i, e: (jnp.minimum(e, n_steps - 1), 0)),
            pl.BlockSpec((d, pair), lambda i, e: (0, jnp.maximum(2 * e - 1, 0))),
            pl.BlockSpec((d, pair), lambda i, e: (0, jnp.minimum(2 * e, 2 * n_steps - 1))),
            pl.BlockSpec((2, nh, n_keys, tm), lambda i, e: (0, 0, 0, i)),
            pl.BlockSpec((2, nh, n_keys, tm), lambda i, e: (0, 0, 0, i)),
        ],
        out_specs=pl.BlockSpec((tm, d), lambda i, e: (i, 0)),
        out_shape=jax.ShapeDtypeStruct((m, d), F32),
        scratch_shapes=([pltpu.VMEM((d, tm), F32)] + [pltpu.VMEM((PEER_SUB, tm), F32)] * n_slots
                        + [pltpu.VMEM((PEER_SUB, tm), BF16)] * n_slots),
        compiler_params=_cparams(("parallel", "arbitrary")),
        name="peerdense",
    )(xt, exp_u, exp_vt, exp_vt, sel_rows, sel_tiles)


def _ln2_kernel(x_ref, p_ref, g_ref, b_ref, y_ref):
    y_ref[...] = _layer_norm(DEEPNORM_ALPHA * x_ref[...] + p_ref[...], g_ref[...], b_ref[...])


def _ln2(x, p, g, b):
    m, d = x.shape
    tm = min(512, m)
    row = lambda i: (i, 0)
    fixed = lambda i: (0, 0)
    return pl.pallas_call(
        _ln2_kernel,
        grid=(m // tm,),
        in_specs=[pl.BlockSpec((tm, d), row), pl.BlockSpec((tm, d), row),
                  pl.BlockSpec((1, d), fixed), pl.BlockSpec((1, d), fixed)],
        out_specs=pl.BlockSpec((tm, d), row),
        out_shape=jax.ShapeDtypeStruct((m, d), F32),
        compiler_params=_cparams(("parallel",)),
        name="ln2",
    )(x, p, g, b)


def _prepare_weights(ln_in_g, ln_in_b, w_in, dn_conv_w, dn_a_log, dn_dt_bias, dn_norm_w, w_out, ln1_g, ln1_b,
                     peer_wq, peer_keys, peer_u, peer_v, ln2_g, ln2_b):
    d = w_in.shape[0]
    o_ak = ATTN_WIDTH
    o_av = o_ak + ATTN_KV_WIDTH
    o_dqkv = o_av + ATTN_KV_WIDTH
    o_dz = o_dqkv + DN_CONV_DIM
    o_db = o_dz + DN_WIDTH
    w_main = jnp.concatenate(
        [w_in[:, :o_ak], w_in[:, o_dz:o_db], w_in[:, o_dqkv:o_dz], w_in[:, o_ak:o_av], w_in[:, o_av:o_dqkv]],
        axis=1).astype(BF16)
    n_gate = 4 * DN_V_HEADS
    w_tail = jnp.concatenate([w_in[:, o_db:o_db + n_gate], jnp.zeros((d, TAIL_WIDTH - n_gate), w_in.dtype)],
                             axis=1).astype(BF16)
    conv_w = jnp.concatenate([dn_conv_w, jnp.zeros((SUBLANES - CONV_W, DN_CONV_DIM), dn_conv_w.dtype)], axis=0)
    params = jnp.zeros((SUBLANES, LANES), F32)
    params = params.at[0:2, 0:DN_V_HEADS].set(dn_a_log).at[2:4, 0:DN_V_HEADS].set(dn_dt_bias)
    n_heads, _, n_keys, half = peer_keys.shape
    row = lambda v: v.reshape(1, -1)
    return dict(
        ln_in_g=row(ln_in_g), ln_in_b=row(ln_in_b), w_main=w_main, w_tail=w_tail, conv_w=conv_w, dn_params=params,
        dn_norm_w=row(dn_norm_w), w_out=w_out.astype(BF16), ln1_g=row(ln1_g), ln1_b=row(ln1_b),
        peer_wq=peer_wq.astype(BF16), peer_keys=peer_keys.reshape(n_heads * 2, n_keys, half).astype(BF16),
        peer_u=peer_u.astype(BF16), peer_vt=peer_v.astype(BF16).T, ln2_g=row(ln2_g), ln2_b=row(ln2_b))


def _trunk(x, w, q_norm_w, k_norm_w):
    b, t, d = x.shape
    m = b * t
    xf = x.reshape(m, d)
    proj, gates = _inproj(xf, w["ln_in_g"], w["ln_in_b"], w["w_main"], w["w_tail"])
    proj3 = proj.reshape(b, t, MAIN_WIDTH)
    o_attn = _attention(*_qkprep(proj3, q_norm_w, k_norm_w))
    dq, dk, dv = _dnprep(proj3, w["conv_w"])
    o_f, o_b = _dnscan(dq, dk, dv, gates.reshape(b, t, TAIL_WIDTH), w["dn_params"])
    x1, x1b, x1bt = _outproj(xf, w["ln_in_g"], w["ln_in_b"], o_attn, o_f.reshape(m, DN_WIDTH),
                             o_b.reshape(m, DN_WIDTH), proj, w["dn_norm_w"], w["w_out"], w["ln1_g"], w["ln1_b"])
    n_sets, n_keys, _ = w["peer_keys"].shape
    s4 = _peerq(x1b, w["peer_wq"], w["peer_keys"]).reshape(n_sets // 2, 2, n_keys, m)
    peer = _peerdense(x1bt, w["peer_u"], w["peer_vt"], *_peertopk(s4))
    y = _ln2(x1, peer, w["ln2_g"], w["ln2_b"])
    return y.reshape(b, t, d)


def kernel(x_prompt, x_sample, ln_in_g, ln_in_b, w_in, q_norm_w, k_norm_w, dn_conv_w, dn_a_log, dn_dt_bias,
           dn_norm_w, w_out, ln1_g, ln1_b, peer_wq, peer_keys, peer_u, peer_v, ln2_g, ln2_b):
    w = _prepare_weights(ln_in_g, ln_in_b, w_in[0], dn_conv_w[0], dn_a_log[0], dn_dt_bias[0], dn_norm_w[0],
                         w_out[0], ln1_g[0], ln1_b[0], peer_wq[0], peer_keys[0], peer_u[0], peer_v[0],
                         ln2_g[0], ln2_b[0])
    return (_trunk(x_prompt, w, q_norm_w[0], k_norm_w[0]), _trunk(x_sample, w, q_norm_w[0], k_norm_w[0]))
```

```python
import functools
import math

import jax
import jax.numpy as jnp
from jax import lax
from jax.experimental import pallas as pl
from jax.experimental.pallas import tpu as pltpu

F32 = jnp.float32
BF16 = jnp.bfloat16

HEAD_DIM = 128
ATTN_HEADS = 8
ATTN_KV_HEADS = 2
ATTN_GROUP = ATTN_HEADS // ATTN_KV_HEADS
ATTN_WIDTH = ATTN_HEADS * HEAD_DIM
ATTN_KV_WIDTH = ATTN_KV_HEADS * HEAD_DIM
DN_QK_HEADS = 4
DN_V_HEADS = 8
DN_DK = 128
DN_DV = 128
DN_QK_WIDTH = DN_QK_HEADS * DN_DK
DN_WIDTH = DN_V_HEADS * DN_DV
DN_CONV_DIM = 2 * DN_QK_WIDTH + DN_WIDTH
GRID_W = 64
ROPE_THETA = 10000.0
CONV_W = 5
PEER_HEADS = 8
PEER_TOPK = 16
DEPTH = 1
DEEPNORM_ALPHA = (2.0 * DEPTH) ** 0.25
LN_EPS = 1e-5
RMS_EPS = 1e-6
LOG2_E = math.log2(math.e)

LANES = 128
SUBLANES = 8
VMEM_LIMIT_BYTES = 56 * 1024 * 1024

MAIN_WIDTH = ATTN_WIDTH + DN_WIDTH + DN_CONV_DIM + 2 * ATTN_KV_WIDTH
TAIL_WIDTH = LANES
PEER_SUB = 256
PEER_NO_RANK = 64.0
PEER_TILE_ROWS = 16
V_ONES_ROWS = 16
DN_CHUNK = 128

NT_DIMS = (((1,), (1,)), ((), ()))
TN_DIMS = (((0,), (0,)), ((), ()))


def _cparams(sem):
    return pltpu.CompilerParams(dimension_semantics=sem, vmem_limit_bytes=VMEM_LIMIT_BYTES)


def _layer_norm(x, g, b):
    mu = jnp.mean(x, axis=-1, keepdims=True)
    xc = x - mu
    var = jnp.mean(xc * xc, axis=-1, keepdims=True)
    return xc * lax.rsqrt(var + LN_EPS) * g + b


def _sigmoid(x):
    return 1.0 / (1.0 + jnp.exp(-x))


def _bdot(a, b, dims=None):
    a = a.astype(BF16)
    b = b.astype(BF16)
    if dims is None:
        return jnp.dot(a, b, preferred_element_type=F32)
    return lax.dot_general(a, b, dims, preferred_element_type=F32)


def _inproj_kernel(x_ref, g_ref, b_ref, w_ref, wt_ref, o_ref, ot_ref, xn_ref):
    @pl.when(pl.program_id(1) == 0)
    def _():
        xn = _layer_norm(x_ref[...], g_ref[...], b_ref[...]).astype(BF16)
        xn_ref[...] = xn
        ot_ref[...] = jnp.dot(xn, wt_ref[...], preferred_element_type=F32)

    o_ref[...] = jnp.dot(xn_ref[...], w_ref[...], preferred_element_type=F32)


def _inproj(x, g, b, w_main, w_tail):
    m, d = x.shape
    tm = min(1024, m)
    tn = 512
    return pl.pallas_call(
        _inproj_kernel,
        grid=(m // tm, MAIN_WIDTH // tn),
        in_specs=[
            pl.BlockSpec((tm, d), lambda i, j: (i, 0)),
            pl.BlockSpec((1, d), lambda i, j: (0, 0)),
            pl.BlockSpec((1, d), lambda i, j: (0, 0)),
            pl.BlockSpec((d, tn), lambda i, j: (0, j)),
            pl.BlockSpec((d, TAIL_WIDTH), lambda i, j: (0, 0)),
        ],
        out_specs=[
            pl.BlockSpec((tm, tn), lambda i, j: (i, j)),
            pl.BlockSpec((tm, TAIL_WIDTH), lambda i, j: (i, 0)),
        ],
        out_shape=[
            jax.ShapeDtypeStruct((m, MAIN_WIDTH), F32),
            jax.ShapeDtypeStruct((m, TAIL_WIDTH), F32),
        ],
        scratch_shapes=[pltpu.VMEM((tm, d), BF16)],
        compiler_params=_cparams(("parallel", "arbitrary")),
        name="inproj",
    )(x, g, b, w_main, w_tail)


def _qkprep_kernel(aq_ref, ak_ref, av_ref, cos_ref, sin_ref, qw_ref, kw_ref, q_ref, k_ref, v_ref):
    cos = cos_ref[...]
    sin = sin_ref[...]
    lane = lax.broadcasted_iota(jnp.int32, cos.shape, 1)
    first = (lane % (HEAD_DIM // 2)) < (HEAD_DIM // 4)

    def prep(x, w, scale):
        ms = jnp.mean(x * x, axis=-1, keepdims=True)
        y = x * lax.rsqrt(ms + RMS_EPS) * w
        sw = jnp.where(first, pltpu.roll(y, LANES - HEAD_DIM // 4, 1), pltpu.roll(y, HEAD_DIM // 4, 1))
        r = y * cos + sw * sin
        return r * scale

    for h in range(ATTN_HEADS):
        sl = slice(h * HEAD_DIM, (h + 1) * HEAD_DIM)
        q_ref[h] = prep(aq_ref[:, sl], qw_ref[...], HEAD_DIM ** -0.5 * LOG2_E).T.astype(BF16)
    for h in range(ATTN_KV_HEADS):
        sl = slice(h * HEAD_DIM, (h + 1) * HEAD_DIM)
        k_ref[:, sl] = prep(ak_ref[:, sl], kw_ref[...], 1.0).astype(BF16)
        v_ref[h, 0:HEAD_DIM, :] = av_ref[:, sl].T.astype(BF16)
        v_ref[h, HEAD_DIM:, :] = jnp.ones((V_ONES_ROWS, v_ref.shape[2]), BF16)


def _rope_tables(t):
    half = HEAD_DIM // 4
    inv = ROPE_THETA ** (-jnp.arange(half, dtype=F32) * (2.0 / (HEAD_DIM // 2)))
    pos = jnp.arange(t, dtype=jnp.int32)
    rows = (pos // GRID_W).astype(F32)[:, None] * inv[None, :]
    cols = (pos % GRID_W).astype(F32)[:, None] * inv[None, :]
    cos = jnp.concatenate([jnp.cos(rows), jnp.cos(rows), jnp.cos(cols), jnp.cos(cols)], axis=-1)
    sin = jnp.concatenate([-jnp.sin(rows), jnp.sin(rows), -jnp.sin(cols), jnp.sin(cols)], axis=-1)
    return cos, sin


def _qkprep(proj3, q_norm_w, k_norm_w):
    b, t, _ = proj3.shape
    tm = min(512, t)
    cos, sin = _rope_tables(t)
    kv_blk = (MAIN_WIDTH - 2 * ATTN_KV_WIDTH) // ATTN_KV_WIDTH
    return pl.pallas_call(
        _qkprep_kernel,
        grid=(b, t // tm),
        in_specs=[
            pl.BlockSpec((None, tm, ATTN_WIDTH), lambda bi, i: (bi, i, 0)),
            pl.BlockSpec((None, tm, ATTN_KV_WIDTH), lambda bi, i: (bi, i, kv_blk)),
            pl.BlockSpec((None, tm, ATTN_KV_WIDTH), lambda bi, i: (bi, i, kv_blk + 1)),
            pl.BlockSpec((tm, HEAD_DIM), lambda bi, i: (i, 0)),
            pl.BlockSpec((tm, HEAD_DIM), lambda bi, i: (i, 0)),
            pl.BlockSpec((1, HEAD_DIM), lambda bi, i: (0, 0)),
            pl.BlockSpec((1, HEAD_DIM), lambda bi, i: (0, 0)),
        ],
        out_specs=[
            pl.BlockSpec((None, ATTN_HEADS, HEAD_DIM, tm), lambda bi, i: (bi, 0, 0, i)),
            pl.BlockSpec((None, tm, ATTN_KV_WIDTH), lambda bi, i: (bi, i, 0)),
            pl.BlockSpec((None, ATTN_KV_HEADS, HEAD_DIM + V_ONES_ROWS, tm), lambda bi, i: (bi, 0, 0, i)),
        ],
        out_shape=[
            jax.ShapeDtypeStruct((b, ATTN_HEADS, HEAD_DIM, t), BF16),
            jax.ShapeDtypeStruct((b, t, ATTN_KV_WIDTH), BF16),
            jax.ShapeDtypeStruct((b, ATTN_KV_HEADS, HEAD_DIM + V_ONES_ROWS, t), BF16),
        ],
        compiler_params=_cparams(("parallel", "parallel")),
        name="qkprep",
    )(proj3, proj3, proj3, cos, sin, q_norm_w.reshape(1, HEAD_DIM), k_norm_w.reshape(1, HEAD_DIM))


def _attn_kernel(qt_ref, k_ref, vt_ref, o_ref, m_ref, acc_ref, *, tk, cpb, nk):
    m_ref[...] = jnp.full(m_ref.shape, -jnp.inf, F32)
    acc_ref[...] = jnp.zeros(acc_ref.shape, F32)

    def body(c, carry):
        ks, vts = [], []
        for j in range(cpb):
            off = pl.multiple_of((c * cpb + j) * tk, tk)
            ks.append(k_ref[pl.ds(off, tk), :])
            vts.append(vt_ref[:, pl.ds(off, tk)])

        def scores(j, g):
            return jnp.dot(ks[j], qt_ref[g], preferred_element_type=F32)

        def softmax(g, s):
            m_prev = m_ref[g]
            m_new = jnp.maximum(m_prev, jnp.max(s, axis=0, keepdims=True))
            m_ref[g] = m_new
            return jnp.exp2(m_prev - m_new), jnp.exp2((s - m_new).astype(BF16))

        def values(j, g, alpha, p):
            acc_ref[g] = acc_ref[g] * alpha + jnp.dot(vts[j], p, preferred_element_type=F32)

        tiles = [(j, g) for j in range(cpb) for g in range(ATTN_GROUP)]
        s = {i: scores(*tiles[i]) for i in range(2)}
        ap = {}
        for i, (j, g) in enumerate(tiles):
            ap[i] = softmax(g, s.pop(i))
            if i + 2 < len(tiles):
                s[i + 2] = scores(*tiles[i + 2])
            if i >= 1:
                values(*tiles[i - 1], *ap.pop(i - 1))
        values(*tiles[-1], *ap.pop(len(tiles) - 1))
        return carry

    lax.fori_loop(0, nk // cpb, body, 0)
    for g in range(ATTN_GROUP):
        out = acc_ref[g, 0:HEAD_DIM, :] * (1.0 / acc_ref[g, HEAD_DIM:HEAD_DIM + 1, :])
        o_ref[:, g * HEAD_DIM:(g + 1) * HEAD_DIM] = out.T.astype(o_ref.dtype)


def _attention(qt, k, vt):
    b, _, _, t = qt.shape
    vrows = vt.shape[2]
    tq = min(256, t)
    tk = min(512, t)
    nk = t // tk
    cpb = math.gcd(nk, 8)
    gw = ATTN_GROUP * HEAD_DIM
    out = pl.pallas_call(
        functools.partial(_attn_kernel, tk=tk, cpb=cpb, nk=nk),
        grid=(b, ATTN_KV_HEADS, t // tq),
        in_specs=[
            pl.BlockSpec((None, ATTN_GROUP, HEAD_DIM, tq), lambda bi, kv, qi: (bi, kv, 0, qi)),
            pl.BlockSpec((None, t, HEAD_DIM), lambda bi, kv, qi: (bi, 0, kv)),
            pl.BlockSpec((None, None, vrows, t), lambda bi, kv, qi: (bi, kv, 0, 0)),
        ],
        out_specs=pl.BlockSpec((None, tq, gw), lambda bi, kv, qi: (bi, qi, kv)),
        out_shape=jax.ShapeDtypeStruct((b, t, ATTN_WIDTH), BF16),
        scratch_shapes=[
            pltpu.VMEM((ATTN_GROUP, 1, tq), F32),
            pltpu.VMEM((ATTN_GROUP, vrows, tq), F32),
        ],
        compiler_params=_cparams(("parallel", "parallel", "arbitrary")),
        name="attn",
    )(qt, k, vt)
    return out.reshape(b * t, ATTN_WIDTH)


def _dnprep_kernel(x_ref, prev_ref, next_ref, cw_ref, q_ref, k_ref, v_ref, xe_ref, *, tt, nt):
    i = pl.program_id(1)
    halo = SUBLANES
    pad = CONV_W // 2
    xe_ref[0:halo, :] = jnp.where(i > 0, prev_ref[...], 0.0)
    xe_ref[halo:halo + tt, :] = x_ref[...]
    xe_ref[halo + tt:halo + tt + halo, :] = jnp.where(i < nt - 1, next_ref[...], 0.0)
    n_qk = DN_QK_WIDTH // DN_DK
    for c in range(DN_CONV_DIM // LANES):
        sl = slice(c * LANES, (c + 1) * LANES)
        acc = xe_ref[halo - pad:halo - pad + tt, sl] * cw_ref[0:1, sl]
        for w in range(1, CONV_W):
            acc = acc + xe_ref[halo - pad + w:halo - pad + w + tt, sl] * cw_ref[w:w + 1, sl]
        h = acc * _sigmoid(acc)
        if c < 2 * n_qk:
            h = h * lax.rsqrt(jnp.sum(h * h, axis=-1, keepdims=True) + RMS_EPS)
        if c < n_qk:
            q_ref[:, sl] = h * (DN_DK ** -0.5)
        elif c < 2 * n_qk:
            k_ref[:, (c - n_qk) * LANES:(c - n_qk + 1) * LANES] = h
        else:
            v_ref[:, (c - 2 * n_qk) * LANES:(c - 2 * n_qk + 1) * LANES] = h


def _dnprep(proj3, conv_w):
    b, t, _ = proj3.shape
    tt = min(256, t)
    nt = t // tt
    hb = tt // SUBLANES
    blk = (ATTN_WIDTH + DN_WIDTH) // DN_CONV_DIM
    return pl.pallas_call(
        functools.partial(_dnprep_kernel, tt=tt, nt=nt),
        grid=(b, nt),
        in_specs=[
            pl.BlockSpec((None, tt, DN_CONV_DIM), lambda bi, i: (bi, i, blk)),
            pl.BlockSpec((None, SUBLANES, DN_CONV_DIM), lambda bi, i: (bi, jnp.maximum(i * hb - 1, 0), blk)),
            pl.BlockSpec((None, SUBLANES, DN_CONV_DIM),
                         lambda bi, i: (bi, jnp.minimum((i + 1) * hb, t // SUBLANES - 1), blk)),
            pl.BlockSpec((SUBLANES, DN_CONV_DIM), lambda bi, i: (0, 0)),
        ],
        out_specs=[
            pl.BlockSpec((None, tt, DN_QK_WIDTH), lambda bi, i: (bi, i, 0)),
            pl.BlockSpec((None, tt, DN_QK_WIDTH), lambda bi, i: (bi, i, 0)),
            pl.BlockSpec((None, tt, DN_WIDTH), lambda bi, i: (bi, i, 0)),
        ],
        out_shape=[
            jax.ShapeDtypeStruct((b, t, DN_QK_WIDTH), F32),
            jax.ShapeDtypeStruct((b, t, DN_QK_WIDTH), F32),
            jax.ShapeDtypeStruct((b, t, DN_WIDTH), F32),
        ],
        scratch_shapes=[pltpu.VMEM((tt + 2 * SUBLANES, DN_CONV_DIM), F32)],
        compiler_params=_cparams(("parallel", "arbitrary")),
        name="dnprep",
    )(proj3, proj3, proj3, conv_w)


def _pick_lane(x, idx):
    lane = lax.broadcasted_iota(jnp.int32, x.shape, x.ndim - 1)
    return jnp.sum(jnp.where(lane == idx, x, 0.0), axis=-1, keepdims=True)


def _exact_tri_dot(tri, x):
    hi = x.astype(BF16)
    r1 = x - hi.astype(F32)
    mid = r1.astype(BF16)
    lo = (r1 - mid.astype(F32)).astype(BF16)
    t = tri.astype(BF16)
    out = jnp.dot(t, hi, preferred_element_type=F32)
    out = out + jnp.dot(t, mid, preferred_element_type=F32)
    return out + jnp.dot(t, lo, preferred_element_type=F32)


def _dnscan_kernel(qf_ref, kf_ref, vf_ref, gf_ref, qb_ref, kb_ref, vb_ref, gb_ref, par_ref,
                   of_ref, ob_ref, sf_ref, sb_ref, *, nc):
    h = pl.program_id(1)
    c = DN_CHUNK

    @pl.when(pl.program_id(2) == 0)
    def _():
        sf_ref[...] = jnp.zeros(sf_ref.shape, F32)
        sb_ref[...] = jnp.zeros(sb_ref.shape, F32)

    row = lax.broadcasted_iota(jnp.int32, (c, c), 0)
    col = lax.broadcasted_iota(jnp.int32, (c, c), 1)
    incl = (col <= row, col >= row)
    strict = (col < row, col > row)
    last = (c - 1, 0)
    refs = ((qf_ref, kf_ref, vf_ref, gf_ref, of_ref, sf_ref), (qb_ref, kb_ref, vb_ref, gb_ref, ob_ref, sb_ref))
    par = par_ref[...]

    units = []
    gcs = {}
    betas = {}
    for d, (q_ref, k_ref, v_ref, g_ref, o_ref, s_ref) in enumerate(refs):
        a_log = _pick_lane(par[d:d + 1, :], h)
        dt_bias = _pick_lane(par[2 + d:3 + d, :], h)
        gates = g_ref[...]
        beta = _sigmoid(_pick_lane(gates, d * DN_V_HEADS + h))
        da = _pick_lane(gates, (2 + d) * DN_V_HEADS + h) + dt_bias
        softplus = jnp.maximum(da, 0.0) + jnp.log(1.0 + jnp.exp(-jnp.abs(da)))
        g = -jnp.exp(a_log) * softplus
        g_cols = jnp.zeros((c, c), F32)
        for ci in range(nc):
            g_cols = jnp.where(col == ci, g[ci * c:(ci + 1) * c, :], g_cols)
        gc_all = _exact_tri_dot(incl[d], g_cols)
        for j in range(nc):
            ci = nc - 1 - j if d else j
            units.append((d, ci))
            gcs[d, ci] = jnp.broadcast_to(gc_all[:, ci:ci + 1], (c, c))
            betas[d, ci] = beta[ci * c:(ci + 1) * c, :]

    def rows(ref, u):
        return ref[u[1] * c:(u[1] + 1) * c, :]

    q = {u: rows(refs[u[0]][0], u) for u in units}
    k = {u: rows(refs[u[0]][1], u) for u in units}
    v = {u: rows(refs[u[0]][2], u) for u in units}
    decay = {u: jnp.exp(jnp.where(incl[u[0]], gcs[u] - gcs[u].T, -jnp.inf)) for u in units}
    eg = {u: jnp.exp(gcs[u]) for u in units}
    g_last = {u: gcs[u][last[u[0]]:last[u[0]] + 1, :] for u in units}
    kb = {u: k[u] * betas[u] for u in units}
    a = {u: jnp.where(strict[u[0]], -(_bdot(kb[u], k[u], NT_DIMS) * decay[u]), 0.0) for u in units}
    a_intra = {u: _bdot(q[u], k[u], NT_DIMS) * decay[u] for u in units}
    x = dict(a)
    for _ in range(int(math.log2(c)) - 1):
        a = {u: _bdot(a[u], a[u]) for u in units}
        x = {u: x[u] + a[u] + _bdot(x[u], a[u]) for u in units}
    uw = {}
    for u in units:
        rhs = jnp.concatenate([v[u] * betas[u], kb[u] * eg[u]], axis=1)
        uw[u] = rhs + _bdot(x[u], rhs)
    wq = {u: jnp.concatenate([uw[u][:, DN_DV:], q[u] * eg[u]], axis=0) for u in units}
    k_tail = {u: k[u] * jnp.exp(g_last[u] - gcs[u]) for u in units}
    s_decay = {u: jnp.exp(g_last[u]) for u in units}

    for j in range(nc):
        step = [(0, j), (1, nc - 1 - j)]
        s = {u: refs[u[0]][5][...] for u in step}
        r = {u: _bdot(wq[u], s[u]) for u in step}
        v_new = {u: uw[u][:, :DN_DV] - r[u][:c] for u in step}
        for u in step:
            refs[u[0]][4][u[1] * c:(u[1] + 1) * c, :] = r[u][c:] + _bdot(a_intra[u], v_new[u])
        for u in step:
            refs[u[0]][5][...] = s[u] * s_decay[u] + _bdot(k_tail[u], v_new[u], TN_DIMS)


def _dnscan(dq, dk, dv, gates3, params):
    b, t, _ = dv.shape
    tt = min(8 * DN_CHUNK, t)
    nt = t // tt
    rep = DN_V_HEADS // DN_QK_HEADS
    fwd_qk = pl.BlockSpec((None, tt, DN_DK), lambda bi, h, i: (bi, i, h // rep))
    bwd_qk = pl.BlockSpec((None, tt, DN_DK), lambda bi, h, i: (bi, nt - 1 - i, h // rep))
    fwd_v = pl.BlockSpec((None, tt, DN_DV), lambda bi, h, i: (bi, i, h))
    bwd_v = pl.BlockSpec((None, tt, DN_DV), lambda bi, h, i: (bi, nt - 1 - i, h))
    fwd_g = pl.BlockSpec((None, tt, TAIL_WIDTH), lambda bi, h, i: (bi, i, 0))
    bwd_g = pl.BlockSpec((None, tt, TAIL_WIDTH), lambda bi, h, i: (bi, nt - 1 - i, 0))
    return pl.pallas_call(
        functools.partial(_dnscan_kernel, nc=tt // DN_CHUNK),
        grid=(b, DN_V_HEADS, nt),
        in_specs=[fwd_qk, fwd_qk, fwd_v, fwd_g, bwd_qk, bwd_qk, bwd_v, bwd_g,
                  pl.BlockSpec((SUBLANES, LANES), lambda bi, h, i: (0, 0))],
        out_specs=[fwd_v, bwd_v],
        out_shape=[jax.ShapeDtypeStruct((b, t, DN_WIDTH), F32)] * 2,
        scratch_shapes=[pltpu.VMEM((DN_DK, DN_DV), F32)] * 2,
        compiler_params=_cparams(("parallel", "parallel", "arbitrary")),
        name="dnscan",
    )(dq, dk, dv, gates3, dq, dk, dv, gates3, params)


def _outproj_kernel(x_ref, g0_ref, b0_ref, oa_ref, of_ref, ob_ref, dz_ref, nw_ref, w_ref, g1_ref, b1_ref,
                    y_ref, yb_ref, ybt_ref):
    o = of_ref[...] + ob_ref[...]
    heads = []
    for h in range(DN_V_HEADS):
        sl = slice(h * DN_DV, (h + 1) * DN_DV)
        oh = o[:, sl]
        oh = oh * lax.rsqrt(jnp.mean(oh * oh, axis=-1, keepdims=True) + RMS_EPS) * nw_ref[...]
        z = dz_ref[:, sl]
        heads.append((oh * (z * _sigmoid(z))).astype(BF16))
    o_dn = jnp.concatenate(heads, axis=1)
    mix = jnp.dot(oa_ref[...], w_ref[0:ATTN_WIDTH, :], preferred_element_type=F32)
    mix = mix + jnp.dot(o_dn, w_ref[ATTN_WIDTH:ATTN_WIDTH + DN_WIDTH, :], preferred_element_type=F32)
    x0 = _layer_norm(x_ref[...], g0_ref[...], b0_ref[...])
    y = _layer_norm(DEEPNORM_ALPHA * x0 + mix, g1_ref[...], b1_ref[...])
    y_ref[...] = y
    yb_ref[...] = y.astype(BF16)
    ybt_ref[...] = y.T.astype(BF16)


def _outproj(x, g0, b0, o_attn, o_f, o_b, proj, norm_w, w_out, g1, b1):
    m, d = x.shape
    tm = min(256, m)
    row = lambda i: (i, 0)
    fixed = lambda i: (0, 0)
    return pl.pallas_call(
        _outproj_kernel,
        grid=(m // tm,),
        in_specs=[
            pl.BlockSpec((tm, d), row),
            pl.BlockSpec((1, d), fixed),
            pl.BlockSpec((1, d), fixed),
            pl.BlockSpec((tm, ATTN_WIDTH), row),
            pl.BlockSpec((tm, DN_WIDTH), row),
            pl.BlockSpec((tm, DN_WIDTH), row),
            pl.BlockSpec((tm, DN_WIDTH), lambda i: (i, ATTN_WIDTH // DN_WIDTH)),
            pl.BlockSpec((1, DN_DV), fixed),
            pl.BlockSpec((ATTN_WIDTH + DN_WIDTH, d), fixed),
            pl.BlockSpec((1, d), fixed),
            pl.BlockSpec((1, d), fixed),
        ],
        out_specs=[pl.BlockSpec((tm, d), row), pl.BlockSpec((tm, d), row), pl.BlockSpec((d, tm), lambda i: (0, i))],
        out_shape=[jax.ShapeDtypeStruct((m, d), F32), jax.ShapeDtypeStruct((m, d), BF16),
                   jax.ShapeDtypeStruct((d, m), BF16)],
        compiler_params=_cparams(("parallel",)),
        name="outproj",
    )(x, g0, b0, o_attn, o_f, o_b, proj, norm_w, w_out, g1, b1)


def _peerq_kernel(x_ref, wq_ref, keys_ref, s_ref, *, n_sets, n_keys, half):
    qry = jnp.dot(x_ref[...], wq_ref[...], preferred_element_type=F32).astype(BF16)
    for hp in range(n_sets):
        s_ref[hp * n_keys:(hp + 1) * n_keys, :] = lax.dot_general(
            keys_ref[hp], qry[:, hp * half:(hp + 1) * half], NT_DIMS, preferred_element_type=F32)


def _peerq(xb, wq, keys):
    m, d = xb.shape
    n_sets, n_keys, half = keys.shape
    tm = min(512, m)
    return pl.pallas_call(
        functools.partial(_peerq_kernel, n_sets=n_sets, n_keys=n_keys, half=half),
        grid=(m // tm,),
        in_specs=[
            pl.BlockSpec((tm, d), lambda i: (i, 0)),
            pl.BlockSpec((d, n_sets * half), lambda i: (0, 0)),
            pl.BlockSpec((n_sets, n_keys, half), lambda i: (0, 0, 0)),
        ],
        out_specs=pl.BlockSpec((n_sets * n_keys, tm), lambda i: (0, i)),
        out_shape=jax.ShapeDtypeStruct((n_sets * n_keys, m), F32),
        compiler_params=_cparams(("parallel",)),
        name="peerq",
    )(xb, wq, keys)


def _extract_top(work, n, ranked=False):
    tops = []
    rank = jnp.full(work.shape, PEER_NO_RANK, F32) if ranked else None
    for r in range(n):
        mx = jnp.max(work, axis=0, keepdims=True)
        tops.append(mx)
        hit = work == mx
        if ranked:
            rank = jnp.where(hit, float(r + 1), rank)
        work = jnp.where(hit, -jnp.inf, work)
    return (tops, rank) if ranked else tops


def _extract_top_stable(work, order, n):
    tops = []
    rank = jnp.full(work.shape, PEER_NO_RANK, F32)
    for r in range(n):
        mx = jnp.max(work, axis=0, keepdims=True)
        tops.append(mx)
        first = jnp.min(jnp.where(work == mx, order, jnp.inf), axis=0, keepdims=True)
        hit = order == first
        rank = jnp.where(hit, float(r + 1), rank)
        work = jnp.where(hit, -jnp.inf, work)
    return tops, rank


def _pair_candidates(a, b):
    k = PEER_TOPK
    half = k // 2
    tl = a[0].shape[1]
    a_all = jnp.concatenate(a, axis=0)
    b_all = jnp.concatenate(b, axis=0)
    row16 = lax.broadcasted_iota(jnp.int32, (k, tl), 0)
    row8 = lax.broadcasted_iota(jnp.int32, (half, tl), 0)
    cands = [a_all + b[0]]
    flat = [row16 * k]
    for q in range(1, half):
        cands.append(jnp.where(row8 < k // (q + 1), a_all[:half] + b[q], -jnp.inf))
        flat.append(row8 * k + q)
    cands.append(a[0] + b_all[half:])
    flat.append(row8 + half)
    return jnp.concatenate(cands, axis=0), jnp.concatenate(flat, axis=0).astype(F32)


def _count_ge(x, v):
    return jnp.sum(jnp.where(x >= v, 1.0, 0.0), axis=0, keepdims=True)


def _softmax_norm(sel):
    z = jnp.exp(sel[0] - sel[0])
    for r in range(1, len(sel)):
        z = z + jnp.exp(sel[r] - sel[0])
    return 1.0 / z


def _peertopk_kernel(s_ref, rows_ref, tiles_ref):
    k = PEER_TOPK
    half = k // 2
    tied_heads = []
    for h in range(PEER_HEADS):
        s1 = s_ref[h, 0]
        s2 = s_ref[h, 1]
        a = _extract_top(s1, k)
        b, rank2 = _extract_top(s2, k, ranked=True)
        cand, flat = _pair_candidates(a, b)
        sel = _extract_top(cand, k)
        tau = sel[k - 1]
        count = jnp.zeros(s1.shape, F32)
        for q in range(k):
            count = count + jnp.where(s1 + b[q] >= tau, 1.0, 0.0)
        rows_ref[0, h] = count
        rows_ref[1, h] = jnp.exp(s1 - a[0])
        tiles_ref[0, h] = rank2.astype(BF16)
        tiles_ref[1, h] = (jnp.exp(s2 - b[0]) * _softmax_norm(sel)).astype(BF16)

        tied = jnp.maximum(jnp.maximum(jnp.abs(_count_ge(s1, a[k - 1]) - k), jnp.abs(_count_ge(s2, b[k - 1]) - k)),
                           jnp.abs(_count_ge(cand, tau) - k))
        tied_heads.append(tied)
        any_tied = tied if h == 0 else jnp.maximum(any_tied, tied)

    def redo_in_index_order(h):
        @pl.when(jnp.max(tied_heads[h]) > 0.0)
        def _():
            s1 = s_ref[h, 0]
            s2 = s_ref[h, 1]
            key = lax.broadcasted_iota(jnp.int32, s1.shape, 0).astype(F32)
            a, rank1 = _extract_top_stable(s1, key, k)
            b, rank2 = _extract_top_stable(s2, key, k)
            cand, flat = _pair_candidates(a, b)
            sel, crank = _extract_top_stable(cand, flat, k)
            chosen = jnp.where(crank <= float(k), 1.0, 0.0)
            per_rank = chosen[:k]
            low = chosen[k:k + half]
            for q in range(2, half):
                low = low + chosen[k + (q - 1) * half:k + q * half]
            tail = jnp.sum(chosen[k + (half - 1) * half:], axis=0, keepdims=True)
            row = lax.broadcasted_iota(jnp.int32, per_rank.shape, 0)
            per_rank = per_rank + jnp.concatenate([low, jnp.zeros_like(low)], axis=0) + jnp.where(row == 0, tail, 0.0)
            count = jnp.zeros(s1.shape, F32)
            for r in range(k):
                count = jnp.where(rank1 == float(r + 1), per_rank[r:r + 1], count)
            rows_ref[0, h] = count
            rows_ref[1, h] = jnp.exp(s1 - a[0])
            tiles_ref[0, h] = rank2.astype(BF16)
            tiles_ref[1, h] = (jnp.exp(s2 - b[0]) * _softmax_norm(sel)).astype(BF16)

    @pl.when(jnp.max(any_tied) > 0.0)
    def _():
        for h in range(PEER_HEADS):
            redo_in_index_order(h)


def _peertopk(s4):
    nh, _, n_keys, m = s4.shape
    tl = min(256, m)
    blk = lambda i: (0, 0, 0, i)
    return pl.pallas_call(
        _peertopk_kernel,
        grid=(m // tl,),
        in_specs=[pl.BlockSpec((nh, 2, n_keys, tl), blk)],
        out_specs=[pl.BlockSpec((2, nh, n_keys, tl), blk), pl.BlockSpec((2, nh, n_keys, tl), blk)],
        out_shape=[jax.ShapeDtypeStruct((2, nh, n_keys, m), F32), jax.ShapeDtypeStruct((2, nh, n_keys, m), BF16)],
        compiler_params=_cparams(("parallel",)),
        name="peertopk",
    )(s4)


def _peerdense_kernel(xt_ref, u_ref, vta_ref, vtb_ref, rows_ref, tiles_ref, o_ref, acc_ref, *slot_refs,
                      n_keys, n_steps):
    e = pl.program_id(1)
    nh = PEER_HEADS
    sub = PEER_SUB
    n_slots = len(slot_refs) // 2
    pre_refs = slot_refs[:n_slots]
    a_refs = slot_refs[n_slots:]
    n_i = sub // n_keys
    n_total = n_steps * n_slots
    tm = xt_ref.shape[1]

    @pl.when(e == 0)
    def _():
        acc_ref[...] = jnp.zeros(acc_ref.shape, F32)
        for r in pre_refs:
            r[...] = jnp.zeros(r.shape, F32)
        for r in a_refs:
            r[...] = jnp.zeros(r.shape, BF16)

    def gated_act(g, pre_ref, a_ref):
        valid = jnp.logical_and(g >= 0, g < n_total)
        i0 = jnp.clip(g, 0, n_total - 1) * n_i
        tr = PEER_TILE_ROWS

        def row_tile(x):
            return jnp.broadcast_to(x, (tr, tm)).astype(BF16)

        cnt_rows = [[row_tile(jnp.where(valid, rows_ref[0, h, pl.ds(i0 + ii, 1), :], 0.0)) for h in range(nh)]
                    for ii in range(n_i)]
        e1_rows = [[row_tile(rows_ref[1, h, pl.ds(i0 + ii, 1), :]) for h in range(nh)] for ii in range(n_i)]
        for tg in range(tm // LANES):
            lanes = slice(tg * LANES, (tg + 1) * LANES)
            for r0 in range(0, n_keys, tr):
                gates = [jnp.zeros((tr, LANES), BF16) for _ in range(n_i)]
                for h in range(nh):
                    rank2 = tiles_ref[0, h, r0:r0 + tr, lanes]
                    e2 = tiles_ref[1, h, r0:r0 + tr, lanes]
                    for ii in range(n_i):
                        sel = rank2 <= cnt_rows[ii][h][:, lanes]
                        gates[ii] = gates[ii] + jnp.where(sel, e2 * e1_rows[ii][h][:, lanes], jnp.zeros_like(e2))
                for ii in range(n_i):
                    p = pre_ref[ii * n_keys + r0:ii * n_keys + r0 + tr, lanes]
                    act = 0.5 * p * (1.0 + lax.erf(p * (2.0 ** -0.5)))
                    a_ref[ii * n_keys + r0:ii * n_keys + r0 + tr, lanes] = act.astype(BF16) * gates[ii]

    half = n_slots // 2
    for s in range(n_slots):
        g = e * n_slots + s
        pre_refs[s][...] = jnp.dot(u_ref[s * sub:(s + 1) * sub, :], xt_ref[...], preferred_element_type=F32)
        gated_act(g - 1, pre_refs[(s - 1) % n_slots], a_refs[(s - 1) % n_slots])
        vt_ref = vta_ref if s < half else vtb_ref
        vcols = slice((s % half) * sub, (s % half + 1) * sub)
        acc_ref[...] += jnp.dot(vt_ref[:, vcols], a_refs[(s - 2) % n_slots][...], preferred_element_type=F32)

    @pl.when(e == n_steps)
    def _():
        o_ref[...] = acc_ref[...].T


def _peerdense(xt, exp_u, exp_vt, sel_rows, sel_tiles):
    d, m = xt.shape
    n_exp = exp_u.shape[0]
    _, nh, n_keys, _ = sel_rows.shape
    tm = min(512, m)
    n_slots = 4
    eb = n_slots * PEER_SUB
    n_steps = n_exp // eb
    pair = eb // 2
    return pl.pallas_call(
        functools.partial(_peerdense_kernel, n_keys=n_keys, n_steps=n_steps),
        grid=(m // tm, n_steps + 1),
        in_specs=[
            pl.BlockSpec((d, tm), lambda i, e: (0, i)),
            pl.BlockSpec((eb, d), lambda i, e: (jnp.minimum(e, n_steps - 1), 0)),
            pl.BlockSpec((d, pair), lambda i, e: (0, jnp.maximum(2 * e - 1, 0))),
            pl.BlockSpec((d, pair), lambda i, e: (0, jnp.minimum(2 * e, 2 * n_steps - 1))),
            pl.BlockSpec((2, nh, n_keys, tm), lambda i, e: (0, 0, 0, i)),
            pl.BlockSpec((2, nh, n_keys, tm), lambda i, e: (0, 0, 0, i)),
        ],
        out_specs=pl.BlockSpec((tm, d), lambda i, e: (i, 0)),
        out_shape=jax.ShapeDtypeStruct((m, d), F32),
        scratch_shapes=([pltpu.VMEM((d, tm), F32)] + [pltpu.VMEM((PEER_SUB, tm), F32)] * n_slots
                        + [pltpu.VMEM((PEER_SUB, tm), BF16)] * n_slots),
        compiler_params=_cparams(("parallel", "arbitrary")),
        name="peerdense",
    )(xt, exp_u, exp_vt, exp_vt, sel_rows, sel_tiles)


def _ln2_kernel(x_ref, p_ref, g_ref, b_ref, y_ref):
    y_ref[...] = _layer_norm(DEEPNORM_ALPHA * x_ref[...] + p_ref[...], g_ref[...], b_ref[...])


def _ln2(x, p, g, b):
    m, d = x.shape
    tm = min(512, m)
    row = lambda i: (i, 0)
    fixed = lambda i: (0, 0)
    return pl.pallas_call(
        _ln2_kernel,
        grid=(m // tm,),
        in_specs=[pl.BlockSpec((tm, d), row), pl.BlockSpec((tm, d), row),
                  pl.BlockSpec((1, d), fixed), pl.BlockSpec((1, d), fixed)],
        out_specs=pl.BlockSpec((tm, d), row),
        out_shape=jax.ShapeDtypeStruct((m, d), F32),
        compiler_params=_cparams(("parallel",)),
        name="ln2",
    )(x, p, g, b)


def _prepare_weights(ln_in_g, ln_in_b, w_in, dn_conv_w, dn_a_log, dn_dt_bias, dn_norm_w, w_out, ln1_g, ln1_b,
                     peer_wq, peer_keys, peer_u, peer_v, ln2_g, ln2_b):
    d = w_in.shape[0]
    o_ak = ATTN_WIDTH
    o_av = o_ak + ATTN_KV_WIDTH
    o_dqkv = o_av + ATTN_KV_WIDTH
    o_dz = o_dqkv + DN_CONV_DIM
    o_db = o_dz + DN_WIDTH
    w_main = jnp.concatenate(
        [w_in[:, :o_ak], w_in[:, o_dz:o_db], w_in[:, o_dqkv:o_dz], w_in[:, o_ak:o_av], w_in[:, o_av:o_dqkv]],
        axis=1).astype(BF16)
    n_gate = 4 * DN_V_HEADS
    w_tail = jnp.concatenate([w_in[:, o_db:o_db + n_gate], jnp.zeros((d, TAIL_WIDTH - n_gate), w_in.dtype)],
                             axis=1).astype(BF16)
    conv_w = jnp.concatenate([dn_conv_w, jnp.zeros((SUBLANES - CONV_W, DN_CONV_DIM), dn_conv_w.dtype)], axis=0)
    params = jnp.zeros((SUBLANES, LANES), F32)
    params = params.at[0:2, 0:DN_V_HEADS].set(dn_a_log).at[2:4, 0:DN_V_HEADS].set(dn_dt_bias)
    n_heads, _, n_keys, half = peer_keys.shape
    row = lambda v: v.reshape(1, -1)
    return dict(
        ln_in_g=row(ln_in_g), ln_in_b=row(ln_in_b), w_main=w_main, w_tail=w_tail, conv_w=conv_w, dn_params=params,
        dn_norm_w=row(dn_norm_w), w_out=w_out.astype(BF16), ln1_g=row(ln1_g), ln1_b=row(ln1_b),
        peer_wq=peer_wq.astype(BF16), peer_keys=peer_keys.reshape(n_heads * 2, n_keys, half).astype(BF16),
        peer_u=peer_u.astype(BF16), peer_vt=peer_v.astype(BF16).T, ln2_g=row(ln2_g), ln2_b=row(ln2_b))


def _trunk(x, w, q_norm_w, k_norm_w):
    b, t, d = x.shape
    m = b * t
    xf = x.reshape(m, d)
    proj, gates = _inproj(xf, w["ln_in_g"], w["ln_in_b"], w["w_main"], w["w_tail"])
    proj3 = proj.reshape(b, t, MAIN_WIDTH)
    o_attn = _attention(*_qkprep(proj3, q_norm_w, k_norm_w))
    dq, dk, dv = _dnprep(proj3, w["conv_w"])
    o_f, o_b = _dnscan(dq, dk, dv, gates.reshape(b, t, TAIL_WIDTH), w["dn_params"])
    x1, x1b, x1bt = _outproj(xf, w["ln_in_g"], w["ln_in_b"], o_attn, o_f.reshape(m, DN_WIDTH),
                             o_b.reshape(m, DN_WIDTH), proj, w["dn_norm_w"], w["w_out"], w["ln1_g"], w["ln1_b"])
    n_sets, n_keys, _ = w["peer_keys"].shape
    s4 = _peerq(x1b, w["peer_wq"], w["peer_keys"]).reshape(n_sets // 2, 2, n_keys, m)
    peer = _peerdense(x1bt, w["peer_u"], w["peer_vt"], *_peertopk(s4))
    y = _ln2(x1, peer, w["ln2_g"], w["ln2_b"])
    return y.reshape(b, t, d)


def kernel(x_prompt, x_sample, ln_in_g, ln_in_b, w_in, q_norm_w, k_norm_w, dn_conv_w, dn_a_log, dn_dt_bias,
           dn_norm_w, w_out, ln1_g, ln1_b, peer_wq, peer_keys, peer_u, peer_v, ln2_g, ln2_b):
    w = _prepare_weights(ln_in_g, ln_in_b, w_in[0], dn_conv_w[0], dn_a_log[0], dn_dt_bias[0], dn_norm_w[0],
                         w_out[0], ln1_g[0], ln1_b[0], peer_wq[0], peer_keys[0], peer_u[0], peer_v[0],
                         ln2_g[0], ln2_b[0])
    return (_trunk(x_prompt, w, q_norm_w[0], k_norm_w[0]), _trunk(x_sample, w, q_norm_w[0], k_norm_w[0]))
```
